```python
import math
import jax
import jax.numpy as jnp
from jax import lax
import numpy as np

D_MODEL = 1024
BATCH = 8
SEQ = 2048
DEPTH = 4
DEC_BATCH = 128
DEC_SEQ = 1
PAST_LEN = 16384
PAGE_SIZE = 128

N_META = 16
NORM_EPS = 1e-6
GLA_HEADS = 4
GLA_DK = 64
GLA_DV = 128
GLA_QK = GLA_HEADS * GLA_DK
GLA_V = GLA_HEADS * GLA_DV
GLA_GATE_RANK = 16
GLA_TAU = 16.0
GLA_CHUNK = 64
RWKV_HEADS = 8
RWKV_N = 64
RWKV_W = RWKV_HEADS * RWKV_N
RWKV_W_RANK = 64
RWKV_A_RANK = 64
RWKV_G_RANK = 128
RWKV_COLS = 3 * RWKV_W + RWKV_W_RANK + RWKV_A_RANK + RWKV_G_RANK
RWKV_GN_EPS = 64e-5
S5_GROUP = 16
S5_GROUPS = 32
S5_WIDTH = S5_GROUPS * S5_GROUP
S5_STATE = 64
IN_SIZES = (GLA_QK, GLA_QK, GLA_V, GLA_V, GLA_GATE_RANK, RWKV_COLS, S5_WIDTH, D_MODEL, D_MODEL, D_MODEL)
N_IN = sum(IN_SIZES)
RWKV_SIZES = (RWKV_W, RWKV_W, RWKV_W, RWKV_W_RANK, RWKV_A_RANK, RWKV_G_RANK)
D_FF = ((8 * D_MODEL // 3 + 255) // 256) * 256

kernel_name = 'gla_rwkv7_s5_hybrid_step'


def _split(t, sizes):
    idx = [int(i) for i in np.cumsum(sizes)[:-1]]
    return jnp.split(t, idx, axis=-1)


def _rmsnorm(x, g):
    xf = x.astype(jnp.float32)
    xf = xf * lax.rsqrt(jnp.mean(xf * xf, axis=-1, keepdims=True) + NORM_EPS)
    return (xf * g.astype(jnp.float32)).astype(x.dtype)


def _gla_chunks(q, k, v, log_a, s0, chunk):
    B, T, H, DK = q.shape
    n = T // chunk

    def to_chunks(t):
        return jnp.moveaxis(t.reshape(B, n, chunk, H, t.shape[-1]), 1, 0)

    causal = jnp.tril(jnp.ones((chunk, chunk), dtype=bool))[None, :, :, None, None]

    def step(s, inp):
        qc, kc, vc, lc = inp
        b = jnp.cumsum(lc, axis=1)
        o_inter = jnp.einsum('bihd,bhde->bihe', qc * jnp.exp(b), s)
        rel = jnp.where(causal, b[:, :, None] - b[:, None, :], -jnp.inf)
        scores = jnp.sum(qc[:, :, None] * kc[:, None] * jnp.exp(rel), axis=-1)
        o_intra = jnp.einsum('bijh,bjhe->bihe', scores, vc)
        b_last = b[:, -1]
        s_new = jnp.exp(b_last)[..., None] * s + jnp.einsum('bjhd,bjhe->bhde', kc * jnp.exp(b_last[:, None] - b), vc)
        return s_new, o_inter + o_intra

    s_fin, o = lax.scan(step, s0, (to_chunks(q), to_chunks(k), to_chunks(v), to_chunks(log_a)))
    return jnp.moveaxis(o, 0, 1).reshape(B, T, H, v.shape[-1]), s_fin


def _gla_segments(q, k, v, log_a, s0, segments):
    outs = []
    s = s0
    start = 0
    for length, chunk in segments:
        sl = slice(start, start + length)
        o, s = _gla_chunks(q[:, sl], k[:, sl], v[:, sl], log_a[:, sl], s, chunk)
        outs.append(o)
        start += length
    return jnp.concatenate(outs, axis=1), s


def _rwkv7(p, prev, s0, lp):
    B, T, _ = p.shape
    p = p.astype(jnp.float32)
    shifted = jnp.concatenate([prev.astype(jnp.float32)[:, None], p[:, :-1]], axis=1)
    xs = p + (shifted - p) * lp['rwkv_mu']
    r, k, v, zw, za, zg = _split(xs, RWKV_SIZES)
    w_log = -jax.nn.softplus(-(lp['rwkv_w0'] + jnp.tanh(zw) @ lp['rwkv_w2'])) - 0.5
    decay = jnp.exp(-jnp.exp(w_log))
    a = jax.nn.sigmoid(lp['rwkv_a0'] + za @ lp['rwkv_a2'])
    g = jax.nn.sigmoid(zg) @ lp['rwkv_g2']

    def hd(t):
        return t.reshape(B, T, RWKV_HEADS, RWKV_N)

    kk = hd(k * lp['rwkv_k_k'])
    kk = kk * lax.rsqrt(jnp.sum(kk * kk, axis=-1, keepdims=True) + 1e-12)
    k = k * (1.0 + (a - 1.0) * lp['rwkv_k_a'])
    r_h, w_h, k_h, v_h, a_h = hd(r), hd(decay), hd(k), hd(v), hd(a)

    def step(S, inp):
        rt, wt, kt, vt, kkt, at = inp
        S = (S * wt[:, :, None, :]
             - jnp.einsum('bhvk,bhk->bhv', S, kkt)[..., None] * (kkt * at)[:, :, None, :]
             + vt[..., None] * kt[:, :, None, :])
        return S, jnp.einsum('bhvk,bhk->bhv', S, rt)

    def tm(t):
        return jnp.swapaxes(t, 0, 1)

    S, y = lax.scan(step, s0.astype(jnp.float32), (tm(r_h), tm(w_h), tm(k_h), tm(v_h), tm(kk), tm(a_h)))
    y = tm(y)
    mu = jnp.mean(y, axis=-1, keepdims=True)
    var = jnp.mean((y - mu) ** 2, axis=-1, keepdims=True)
    y = ((y - mu) * lax.rsqrt(var + RWKV_GN_EPS)).reshape(B, T, RWKV_W) * lp['rwkv_ln_w'] + lp['rwkv_ln_b']
    bonus = jnp.sum(r_h * k_h * lp['rwkv_r_k'], axis=-1, keepdims=True) * v_h
    y = (y + bonus.reshape(B, T, RWKV_W)) * g
    return y, S, p[:, -1]


def _s5(u, x0_re, x0_im, lp):
    B, T, _ = u.shape
    u = u.astype(jnp.float32)
    ug = u.reshape(B, T, S5_GROUPS, S5_GROUP)
    lam_re, lam_im = lp['s5_a_re'], lp['s5_a_im']
    dt = jnp.exp(lp['s5_log_dt'])[:, None]
    mag = jnp.exp(lam_re * dt)
    abar_re = mag * jnp.cos(lam_im * dt)
    abar_im = mag * jnp.sin(lam_im * dt)
    den = lam_re * lam_re + lam_im * lam_im
    nr = abar_re - 1.0
    coef_re = (nr * lam_re + abar_im * lam_im) / den
    coef_im = (abar_im * lam_re - nr * lam_im) / den
    bbar_re = coef_re[..., None] * lp['s5_b_re'] - coef_im[..., None] * lp['s5_b_im']
    bbar_im = coef_re[..., None] * lp['s5_b_im'] + coef_im[..., None] * lp['s5_b_re']
    bu_re = jnp.einsum('btgh,gph->btgp', ug, bbar_re)
    bu_im = jnp.einsum('btgh,gph->btgp', ug, bbar_im)
    x0_re = x0_re.astype(jnp.float32)
    x0_im = x0_im.astype(jnp.float32)
    bu_re = bu_re.at[:, 0].add(abar_re * x0_re - abar_im * x0_im)
    bu_im = bu_im.at[:, 0].add(abar_re * x0_im + abar_im * x0_re)
    a_re = jnp.broadcast_to(abar_re, bu_re.shape)
    a_im = jnp.broadcast_to(abar_im, bu_im.shape)

    def combine(e1, e2):
        a1r, a1i, b1r, b1i = e1
        a2r, a2i, b2r, b2i = e2
        return (a1r * a2r - a1i * a2i, a1r * a2i + a1i * a2r,
                a2r * b1r - a2i * b1i + b2r, a2r * b1i + a2i * b1r + b2i)

    _, _, xr, xi = lax.associative_scan(combine, (a_re, a_im, bu_re, bu_im), axis=1)
    y = jnp.einsum('btgp,ghp->btgh', xr, lp['s5_c_re']) - jnp.einsum('btgp,ghp->btgh', xi, lp['s5_c_im'])
    y = y.reshape(B, T, S5_WIDTH) + lp['s5_d'] * u
    y = jax.nn.gelu(y)
    y = y * jax.nn.sigmoid(y @ lp['s5_glu_w'] + lp['s5_glu_b'])
    return y, xr[:, -1], xi[:, -1]


def _layer(x, st, lp, segments):
    gla_s0, rwkv_s0, shift0, s5r0, s5i0 = st
    B, T, _ = x.shape
    dt = x.dtype
    f32 = jnp.float32
    h = _rmsnorm(x, lp['norm_mix'])
    proj = h @ lp['w_in']
    q, k, v, g_gla, z_gate, p_rwkv, u_s5, gate_a, gate_b, gate_c = _split(proj, IN_SIZES)
    log_a = jax.nn.log_sigmoid((z_gate @ lp['gla_gate_w2'] + lp['gla_gate_b']).astype(f32)) / GLA_TAU
    qh = q.astype(f32).reshape(B, T, GLA_HEADS, GLA_DK) * (GLA_DK ** -0.5)
    kh = k.astype(f32).reshape(B, T, GLA_HEADS, GLA_DK)
    vh = v.astype(f32).reshape(B, T, GLA_HEADS, GLA_DV)
    o, gla_s = _gla_segments(qh, kh, vh, log_a.reshape(B, T, GLA_HEADS, GLA_DK), gla_s0.astype(f32), segments)
    o = o * lax.rsqrt(jnp.mean(o * o, axis=-1, keepdims=True) + NORM_EPS)
    o_gla = o.reshape(B, T, GLA_V) * lp['gla_norm'] * jax.nn.silu(g_gla.astype(f32))
    o_rwkv, rwkv_s, shift = _rwkv7(p_rwkv, shift0, rwkv_s0, lp)
    o_s5, s5r, s5i = _s5(u_s5, s5r0, s5i0, lp)
    m = (jax.nn.sigmoid(gate_a) * (o_gla.astype(dt) @ lp['w_br_gla'])
         + jax.nn.sigmoid(gate_b) * (o_rwkv.astype(dt) @ lp['w_br_rwkv'])
         + jax.nn.sigmoid(gate_c) * (o_s5.astype(dt) @ lp['w_br_s5']))
    x = x + m @ lp['w_out']
    h2 = _rmsnorm(x, lp['norm_ffn'])
    x = x + (jax.nn.silu(h2 @ lp['ffn_w1']) * (h2 @ lp['ffn_w3'])) @ lp['ffn_w2']
    return x, (gla_s, rwkv_s, shift, s5r, s5i)


def setup_inputs(seed: int = 0) -> dict:
    key = jax.random.key(seed)
    ks = iter(jax.random.split(key, 64))
    f32 = jnp.float32

    def nrm(shape, scale):
        return jax.random.normal(next(ks), shape, f32) * scale

    def gain(shape):
        return 1.0 + nrm(shape, 0.02)

    L = DEPTH
    a_im0 = math.pi * jnp.arange(S5_STATE, dtype=f32)
    return {
        'x_prompt': nrm((BATCH, SEQ, D_MODEL), 1.0),
        'x_sample': nrm((DEC_BATCH, DEC_SEQ, D_MODEL), 1.0),
        'state_gla': nrm((L, DEC_BATCH, GLA_HEADS, GLA_DK, GLA_DV), 0.5),
        'state_rwkv': nrm((L, DEC_BATCH, RWKV_HEADS, RWKV_N, RWKV_N), 0.3),
        'state_rwkv_shift': nrm((L, DEC_BATCH, RWKV_COLS), 1.0),
        'state_s5_re': nrm((L, DEC_BATCH, S5_GROUPS, S5_STATE), 0.5),
        'state_s5_im': nrm((L, DEC_BATCH, S5_GROUPS, S5_STATE), 0.5),
        'meta_tokens': nrm((N_META, D_MODEL), 1.0),
        'norm_mix': gain((L, D_MODEL)),
        'norm_ffn': gain((L, D_MODEL)),
        'w_in': nrm((L, D_MODEL, N_IN), D_MODEL ** -0.5),
        'gla_gate_w2': nrm((L, GLA_GATE_RANK, GLA_QK), GLA_GATE_RANK ** -0.5),
        'gla_gate_b': nrm((L, GLA_QK), 0.1),
        'gla_norm': gain((L, GLA_V)),
        'rwkv_mu': jax.random.uniform(next(ks), (L, RWKV_COLS), f32),
        'rwkv_w0': -2.0 + nrm((L, RWKV_W), 0.5),
        'rwkv_w2': nrm((L, RWKV_W_RANK, RWKV_W), 0.1 * RWKV_W_RANK ** -0.5),
        'rwkv_a0': nrm((L, RWKV_W), 0.1),
        'rwkv_a2': nrm((L, RWKV_A_RANK, RWKV_W), 0.1 * RWKV_A_RANK ** -0.5),
        'rwkv_g2': nrm((L, RWKV_G_RANK, RWKV_W), RWKV_G_RANK ** -0.5),
        'rwkv_k_k': 0.85 + nrm((L, RWKV_W), 0.02),
        'rwkv_k_a': gain((L, RWKV_W)),
        'rwkv_r_k': nrm((L, RWKV_HEADS, RWKV_N), 0.1),
        'rwkv_ln_w': gain((L, RWKV_W)),
        'rwkv_ln_b': nrm((L, RWKV_W), 0.02),
        's5_a_re': -0.5 * jnp.exp(nrm((L, S5_GROUPS, S5_STATE), 0.05)),
        's5_a_im': a_im0 + nrm((L, S5_GROUPS, S5_STATE), 0.01),
        's5_log_dt': jax.random.uniform(next(ks), (L, S5_GROUPS), f32, math.log(1e-3), math.log(1e-1)),
        's5_b_re': nrm((L, S5_GROUPS, S5_STATE, S5_GROUP), (2 * S5_GROUP) ** -0.5),
        's5_b_im': nrm((L, S5_GROUPS, S5_STATE, S5_GROUP), (2 * S5_GROUP) ** -0.5),
        's5_c_re': nrm((L, S5_GROUPS, S5_GROUP, S5_STATE), (2 * S5_STATE) ** -0.5),
        's5_c_im': nrm((L, S5_GROUPS, S5_GROUP, S5_STATE), (2 * S5_STATE) ** -0.5),
        's5_d': nrm((L, S5_WIDTH), 0.5),
        's5_glu_w': nrm((L, S5_WIDTH, S5_WIDTH), S5_WIDTH ** -0.5),
        's5_glu_b': nrm((L, S5_WIDTH), 0.02),
        'w_br_gla': nrm((L, GLA_V, D_MODEL), GLA_V ** -0.5),
        'w_br_rwkv': nrm((L, RWKV_W, D_MODEL), RWKV_W ** -0.5),
        'w_br_s5': nrm((L, S5_WIDTH, D_MODEL), S5_WIDTH ** -0.5),
        'w_out': nrm((L, D_MODEL, D_MODEL), D_MODEL ** -0.5),
        'ffn_w1': nrm((L, D_MODEL, D_FF), D_MODEL ** -0.5),
        'ffn_w3': nrm((L, D_MODEL, D_FF), D_MODEL ** -0.5),
        'ffn_w2': nrm((L, D_FF, D_MODEL), D_FF ** -0.5),
        'final_norm': gain((D_MODEL,)),
    }


def reference(x_prompt, x_sample, state_gla, state_rwkv, state_rwkv_shift, state_s5_re, state_s5_im,
              meta_tokens, norm_mix, norm_ffn, w_in, gla_gate_w2, gla_gate_b, gla_norm,
              rwkv_mu, rwkv_w0, rwkv_w2, rwkv_a0, rwkv_a2, rwkv_g2, rwkv_k_k, rwkv_k_a, rwkv_r_k,
              rwkv_ln_w, rwkv_ln_b, s5_a_re, s5_a_im, s5_log_dt, s5_b_re, s5_b_im, s5_c_re, s5_c_im,
              s5_d, s5_glu_w, s5_glu_b, w_br_gla, w_br_rwkv, w_br_s5, w_out, ffn_w1, ffn_w3, ffn_w2,
              final_norm):
    f32 = jnp.float32
    Bp, Tp, _ = x_prompt.shape
    Ts = x_sample.shape[1]
    meta = jnp.broadcast_to(meta_tokens.astype(x_prompt.dtype)[None], (Bp, N_META, D_MODEL))
    xp = jnp.concatenate([meta, x_prompt], axis=1)
    xs = x_sample
    seg_p = ((N_META, N_META), (Tp, GLA_CHUNK))
    seg_s = ((Ts, Ts),)
    zero_p = (jnp.zeros((Bp, GLA_HEADS, GLA_DK, GLA_DV), f32),
              jnp.zeros((Bp, RWKV_HEADS, RWKV_N, RWKV_N), f32),
              jnp.zeros((Bp, RWKV_COLS), f32),
              jnp.zeros((Bp, S5_GROUPS, S5_STATE), f32),
              jnp.zeros((Bp, S5_GROUPS, S5_STATE), f32))
    new_p = [[], [], [], [], []]
    new_s = [[], [], [], [], []]
    for l in range(DEPTH):
        lp = {
            'norm_mix': norm_mix[l], 'norm_ffn': norm_ffn[l], 'w_in': w_in[l],
            'gla_gate_w2': gla_gate_w2[l], 'gla_gate_b': gla_gate_b[l], 'gla_norm': gla_norm[l],
            'rwkv_mu': rwkv_mu[l], 'rwkv_w0': rwkv_w0[l], 'rwkv_w2': rwkv_w2[l], 'rwkv_a0': rwkv_a0[l],
            'rwkv_a2': rwkv_a2[l], 'rwkv_g2': rwkv_g2[l], 'rwkv_k_k': rwkv_k_k[l], 'rwkv_k_a': rwkv_k_a[l],
            'rwkv_r_k': rwkv_r_k[l], 'rwkv_ln_w': rwkv_ln_w[l], 'rwkv_ln_b': rwkv_ln_b[l],
            's5_a_re': s5_a_re[l], 's5_a_im': s5_a_im[l], 's5_log_dt': s5_log_dt[l],
            's5_b_re': s5_b_re[l], 's5_b_im': s5_b_im[l], 's5_c_re': s5_c_re[l], 's5_c_im': s5_c_im[l],
            's5_d': s5_d[l], 's5_glu_w': s5_glu_w[l], 's5_glu_b': s5_glu_b[l],
            'w_br_gla': w_br_gla[l], 'w_br_rwkv': w_br_rwkv[l], 'w_br_s5': w_br_s5[l], 'w_out': w_out[l],
            'ffn_w1': ffn_w1[l], 'ffn_w3': ffn_w3[l], 'ffn_w2': ffn_w2[l],
        }
        xp, stp = _layer(xp, zero_p, lp, seg_p)
        st_in = (state_gla[l], state_rwkv[l], state_rwkv_shift[l], state_s5_re[l], state_s5_im[l])
        xs, sts = _layer(xs, st_in, lp, seg_s)
        for i in range(5):
            new_p[i].append(stp[i])
            new_s[i].append(sts[i])
    y_prompt = _rmsnorm(xp, final_norm)[:, N_META:]
    y_sample = _rmsnorm(xs, final_norm)
    gla_p, rwkv_p, shift_p, s5re_p, s5im_p = [jnp.stack(t, axis=0) for t in new_p]
    gla_s, rwkv_s, shift_s, s5re_s, s5im_s = [jnp.stack(t, axis=0) for t in new_s]
    return (y_prompt, y_sample, gla_p, rwkv_p, shift_p, s5re_p, s5im_p, gla_s, rwkv_s, shift_s, s5re_s, s5im_s)
```

```python
import functools
import math

import jax
import jax.numpy as jnp
from jax import lax
from jax.experimental import pallas as pl
from jax.experimental.pallas import tpu as pltpu

F32 = jnp.float32
BF16 = jnp.bfloat16
HIGHEST = lax.Precision.HIGHEST

N_META = 16
NORM_EPS = 1e-6
GLA_HEADS = 4
GLA_DK = 64
GLA_DV = 128
GLA_QK = GLA_HEADS * GLA_DK
GLA_V = GLA_HEADS * GLA_DV
GLA_GATE_RANK = 16
GLA_TAU = 16.0
GLA_SUB = 16
RWKV_HEADS = 8
RWKV_N = 64
RWKV_W = RWKV_HEADS * RWKV_N
RWKV_W_RANK = 64
RWKV_A_RANK = 64
RWKV_G_RANK = 128
RWKV_LOW = RWKV_W_RANK + RWKV_A_RANK + RWKV_G_RANK
RWKV_COLS = 3 * RWKV_W + RWKV_LOW
RWKV_GN_EPS = 64e-5
S5_GROUP = 16
S5_GROUPS = 32
S5_WIDTH = S5_GROUPS * S5_GROUP
S5_STATE = 64
S5_N = S5_GROUPS * S5_STATE

LANES = 128
CHAINS = LANES // 2
GLA_ZPAD = LANES
GLA_COLS = 2 * GLA_QK + 2 * GLA_V + GLA_ZPAD
VMEM_LIMIT = 56 << 20


def _params(sem):
    return pltpu.CompilerParams(dimension_semantics=sem, vmem_limit_bytes=VMEM_LIMIT)


def _const_spec(shape):
    nd = len(shape)
    return pl.BlockSpec(shape, lambda *_: (0,) * nd, pipeline_mode=pl.Buffered(1))


def _row_spec(tm, cols, off=0):
    return pl.BlockSpec((tm, cols), lambda i: (i + off, 0))


def _pick_block(n, mult, cap):
    best = None
    for d in range(mult, min(n, cap) + 1, mult):
        if n % d == 0:
            best = d
    assert best is not None, (n, mult, cap)
    return best


def _sigmoid(x):
    return 1.0 / (1.0 + jnp.exp(-x))


def _softplus(x):
    return jnp.maximum(x, 0.0) + jnp.log1p(jnp.exp(-jnp.abs(x)))


def _rms(x, g):
    return x * lax.rsqrt(jnp.mean(x * x, axis=-1, keepdims=True) + NORM_EPS) * g


def _dot(a, b):
    return jnp.dot(a, b, preferred_element_type=F32)


def _dot_t0(a, b):
    return lax.dot_general(a, b, (((0,), (0,)), ((), ())), preferred_element_type=F32)


def _split3_rows(x):
    hi = x.astype(BF16).astype(F32)
    r1 = x - hi
    mid = r1.astype(BF16).astype(F32)
    lo = (r1 - mid).astype(BF16).astype(F32)
    return [hi, mid, lo]


def _inproj_body(x_ref, g_ref, wg_ref, wr_ref, ws_ref, wt_ref, og_ref, or_ref, os_ref, ot_ref):
    h = _rms(x_ref[...], g_ref[...]).astype(BF16)
    og_ref[...] = _dot(h, wg_ref[...])
    or_ref[...] = _dot(h, wr_ref[...])
    os_ref[...] = _dot(h, ws_ref[...])
    ot_ref[...] = _dot(h, wt_ref[...])


def _inproj(x, g, wg, wr, ws, wt, tm):
    m, d = x.shape
    widths = (wg.shape[1], wr.shape[1], ws.shape[1], wt.shape[1])
    return pl.pallas_call(
        _inproj_body,
        grid=(m // tm,),
        in_specs=[_row_spec(tm, d), _const_spec((1, d))] + [_const_spec(w.shape) for w in (wg, wr, ws, wt)],
        out_specs=[_row_spec(tm, c) for c in widths],
        out_shape=[jax.ShapeDtypeStruct((m, c), F32) for c in widths],
        compiler_params=_params(("parallel",)),
        name="inproj",
    )(x, g, wg, wr, ws, wt)


def _log_sigmoid(x):
    return jnp.minimum(x, 0.0) - jnp.log1p(jnp.exp(-jnp.abs(x)))


def _gla_finish(o, g, gn):
    parts = []
    for h in range(GLA_HEADS):
        oh = o[:, h * GLA_DV:(h + 1) * GLA_DV]
        parts.append(oh * lax.rsqrt(jnp.mean(oh * oh, axis=-1, keepdims=True) + NORM_EPS))
    on = jnp.concatenate(parts, axis=1)
    return on * gn * (g * _sigmoid(g))


def _gla_log_decay(z, w2_ref, gb_ref):
    zz = _dot(z.astype(BF16), w2_ref[...]) + gb_ref[...]
    return _log_sigmoid(zz) * (1.0 / GLA_TAU)


def _gla_prompt_body(p_ref, w2_ref, gb_ref, gn_ref, tri_ref, segb_ref, ones_ref, o_ref, sf_ref, s_ref, *, nchunk):
    j = pl.program_id(1)
    c = GLA_SUB

    @pl.when(j == 0)
    def _():
        s_ref[...] = jnp.zeros_like(s_ref)

    lane_head = lax.broadcasted_iota(jnp.int32, (c, GLA_QK), 1) // GLA_DK
    row_id = lax.broadcasted_iota(jnp.int32, (c, GLA_QK), 0)

    def chunk(ci, carry):
        rows = pl.ds(pl.multiple_of(ci * c, c), c)
        q = p_ref[rows, 0:GLA_QK]
        k = p_ref[rows, GLA_QK:2 * GLA_QK]
        v = p_ref[rows, 2 * GLA_QK:2 * GLA_QK + GLA_V]
        g = p_ref[rows, 2 * GLA_QK + GLA_V:2 * GLA_QK + 2 * GLA_V]
        z = p_ref[rows, 2 * GLA_QK + 2 * GLA_V:GLA_COLS]
        la = _gla_log_decay(z, w2_ref, gb_ref)
        bc = jnp.dot(tri_ref[...], la, precision=HIGHEST, preferred_element_type=F32)
        bl = bc[c - 1:c, :]
        qs = q * (GLA_DK ** -0.5)
        s = s_ref[...]
        qe = qs * jnp.exp(bc)
        lhs = jnp.concatenate([jnp.where(lane_head == h, qe, 0.0) for h in range(GLA_HEADS)], axis=0)
        oi = _dot(lhs.astype(BF16), s.astype(BF16))
        o_inter = jnp.concatenate([oi[h * c:(h + 1) * c] for h in range(GLA_HEADS)], axis=1)
        terms = []
        for jj in range(c):
            e = jnp.exp(jnp.minimum(bc - bc[jj:jj + 1, :], 0.0))
            terms.append(jnp.where(row_id >= jj, qs * k[jj:jj + 1, :] * e, 0.0))
        pw = _dot(jnp.concatenate(terms, axis=0).astype(BF16), segb_ref[...])
        o_intra = pw[0:c] * v[0:1, :]
        for jj in range(1, c):
            o_intra = o_intra + pw[jj * c:(jj + 1) * c] * v[jj:jj + 1, :]
        o_ref[rows, :] = _gla_finish(o_inter + o_intra, g, gn_ref[...]).astype(o_ref.dtype)
        kd = k * jnp.exp(bl - bc)
        upd = _dot_t0(kd.astype(BF16), v.astype(BF16))
        kv = jnp.concatenate(
            [upd[h * GLA_DK:(h + 1) * GLA_DK, h * GLA_DV:(h + 1) * GLA_DV] for h in range(GLA_HEADS)], axis=0)
        ef = jnp.concatenate(_split3_rows(jnp.exp(bl)) + [jnp.zeros((c - 3, GLA_QK), F32)], axis=0)
        dec = _dot_t0(ef.astype(BF16), ones_ref[...])
        s_ref[...] = dec * s + kv
        return carry

    lax.fori_loop(0, nchunk, chunk, 0)

    @pl.when(j == pl.num_programs(1) - 1)
    def _():
        sf_ref[0] = s_ref[...]


def _gla_prompt(pg, w2p, gb, gn, consts, nb, t):
    tb = _pick_block(t, GLA_SUB, 1024)
    nt = t // tb
    tri, segb, ones = consts
    return pl.pallas_call(
        functools.partial(_gla_prompt_body, nchunk=tb // GLA_SUB),
        grid=(nb, nt),
        in_specs=[pl.BlockSpec((tb, GLA_COLS), lambda b, j: (b * nt + j, 0)),
                  _const_spec(w2p.shape), _const_spec(gb.shape), _const_spec(gn.shape),
                  _const_spec(tri.shape), _const_spec(segb.shape), _const_spec(ones.shape)],
        out_specs=[pl.BlockSpec((tb, GLA_V), lambda b, j: (b * nt + j, 0)),
                   pl.BlockSpec((1, GLA_QK, GLA_DV), lambda b, j: (b, 0, 0))],
        out_shape=[jax.ShapeDtypeStruct((nb * t, GLA_V), BF16),
                   jax.ShapeDtypeStruct((nb, GLA_QK, GLA_DV), F32)],
        scratch_shapes=[pltpu.VMEM((GLA_QK, GLA_DV), F32)],
        compiler_params=_params(("parallel", "arbitrary")),
        name="gla_prompt",
    )(pg, w2p, gb, gn, tri, segb, ones)


def _gla_sample_body(p_ref, s0_ref, w2_ref, gb_ref, gn_ref, sel_ref, o_ref, s_ref, *, nrow):
    q = p_ref[:, 0:GLA_QK]
    k = p_ref[:, GLA_QK:2 * GLA_QK]
    v = p_ref[:, 2 * GLA_QK:2 * GLA_QK + GLA_V]
    g = p_ref[:, 2 * GLA_QK + GLA_V:2 * GLA_QK + 2 * GLA_V]
    z = p_ref[:, 2 * GLA_QK + 2 * GLA_V:GLA_COLS]
    a = jnp.exp(_gla_log_decay(z, w2_ref, gb_ref))
    qs = q * (GLA_DK ** -0.5)
    out_rows = []
    for i in range(nrow):
        pieces = _split3_rows(a[i:i + 1]) + _split3_rows(k[i:i + 1]) + _split3_rows(qs[i:i + 1])
        ef = jnp.concatenate(pieces + [jnp.zeros((16 - 9, GLA_QK), F32)], axis=0)
        cb = _dot_t0(ef.astype(BF16), sel_ref[...])
        ab = cb[:, 0:GLA_DV]
        kb = cb[:, GLA_DV:2 * GLA_DV]
        qb = cb[:, 2 * GLA_DV:3 * GLA_DV]
        vb = jnp.concatenate(
            [jnp.broadcast_to(v[i:i + 1, h * GLA_DV:(h + 1) * GLA_DV], (GLA_DK, GLA_DV)) for h in range(GLA_HEADS)],
            axis=0)
        s = ab * s0_ref[i] + kb * vb
        s_ref[i] = s
        op = qb * s
        out_rows.append(jnp.concatenate(
            [jnp.sum(op[h * GLA_DK:(h + 1) * GLA_DK], axis=0, keepdims=True) for h in range(GLA_HEADS)], axis=1))
    o = jnp.concatenate(out_rows, axis=0)
    o_ref[...] = _gla_finish(o, g, gn_ref[...]).astype(o_ref.dtype)


def _gla_sample(pg, s0, w2p, gb, gn, sel, row0, ns):
    nrow = 16
    assert ns % nrow == 0 and row0 % nrow == 0
    return pl.pallas_call(
        functools.partial(_gla_sample_body, nrow=nrow),
        grid=(ns // nrow,),
        in_specs=[_row_spec(nrow, GLA_COLS, row0 // nrow),
                  pl.BlockSpec((nrow, GLA_QK, GLA_DV), lambda i: (i, 0, 0)),
                  _const_spec(w2p.shape), _const_spec(gb.shape), _const_spec(gn.shape), _const_spec(sel.shape)],
        out_specs=[_row_spec(nrow, GLA_V), pl.BlockSpec((nrow, GLA_QK, GLA_DV), lambda i: (i, 0, 0))],
        out_shape=[jax.ShapeDtypeStruct((ns, GLA_V), BF16), jax.ShapeDtypeStruct((ns, GLA_QK, GLA_DV), F32)],
        compiler_params=_params(("parallel",)),
        name="gla_sample",
    )(pg, s0, w2p, gb, gn, sel)


def _rwkv_prep_body(p_ref, pp_ref, mu_ref, w0_ref, a0_ref, kkw_ref, kaw_ref, wlow_ref, bd_ref,
                    r_o, w_o, k_o, v_o, kk_o, kka_o, g_o):
    p = p_ref[...]
    xs = p + (pp_ref[...] - p) * mu_ref[...]
    r = xs[:, 0:RWKV_W]
    k = xs[:, RWKV_W:2 * RWKV_W]
    v = xs[:, 2 * RWKV_W:3 * RWKV_W]
    zl = xs[:, 3 * RWKV_W:RWKV_COLS]
    lane = lax.broadcasted_iota(jnp.int32, zl.shape, 1)
    zt = jnp.where(lane < RWKV_W_RANK, jnp.tanh(zl),
                   jnp.where(lane < RWKV_W_RANK + RWKV_A_RANK, zl, _sigmoid(zl)))
    low = _dot(zt.astype(BF16), wlow_ref[...])
    w_log = -_softplus(-(w0_ref[...] + low[:, 0:RWKV_W])) - 0.5
    a = _sigmoid(a0_ref[...] + low[:, RWKV_W:2 * RWKV_W])
    kk = k * kkw_ref[...]
    ss = jnp.dot(kk * kk, bd_ref[...], precision=HIGHEST, preferred_element_type=F32)
    kk = kk * lax.rsqrt(ss + 1e-12)
    r_o[...] = r
    w_o[...] = jnp.exp(-jnp.exp(w_log))
    k_o[...] = k * (1.0 + (a - 1.0) * kaw_ref[...])
    v_o[...] = v
    kk_o[...] = kk
    kka_o[...] = kk * a
    g_o[...] = low[:, 2 * RWKV_W:3 * RWKV_W]


def _rwkv_prep(p, pp, mu, w0, a0, kkw, kaw, wlow, bd, tm):
    m = p.shape[0]
    vec = _const_spec((1, RWKV_W))
    return pl.pallas_call(
        _rwkv_prep_body,
        grid=(m // tm,),
        in_specs=[_row_spec(tm, RWKV_COLS), _row_spec(tm, RWKV_COLS), _const_spec((1, RWKV_COLS)),
                  vec, vec, vec, vec, _const_spec(wlow.shape), _const_spec(bd.shape)],
        out_specs=[_row_spec(tm, RWKV_W)] * 7,
        out_shape=[jax.ShapeDtypeStruct((m, RWKV_W), F32)] * 7,
        compiler_params=_params(("parallel",)),
        name="rwkv_prep",
    )(p, pp, mu, w0, a0, kkw, kaw, wlow, bd)


def _rwkv_scan_body(w_ref, kk_ref, kka_ref, k_ref, r_ref, v_ref, s0_ref, y_ref, sf_ref, s_ref, *, tc):
    j = pl.program_id(1)
    nv2 = RWKV_N // 2

    @pl.when(j == 0)
    def _():
        s_ref[...] = s0_ref[0]

    def token(t, carry):
        w = w_ref[0, t]
        kk = kk_ref[0, t]
        kka = kka_ref[0, t]
        k = k_ref[0, t]
        r = r_ref[0, t]
        for v2 in range(nv2):
            s = s_ref[v2]
            sk = jnp.sum(s * kk, axis=0, keepdims=True)
            vv = v_ref[0, t, v2:v2 + 1, :]
            sn = s * w - sk * kka + vv * k
            s_ref[v2] = sn
            y_ref[0, t, v2:v2 + 1, :] = jnp.sum(sn * r, axis=0, keepdims=True)
        return carry

    lax.fori_loop(0, tc, token, 0)

    @pl.when(j == pl.num_programs(1) - 1)
    def _():
        sf_ref[0] = s_ref[...]


def _rwkv_scan(wt, kkt, kkat, kt, rt, vt, s0):
    g, t = wt.shape[0], wt.shape[1]
    tc = _pick_block(t, 1, 48)
    nv2 = RWKV_N // 2
    big = pl.BlockSpec((1, tc, RWKV_N, LANES), lambda i, j: (i, j, 0, 0))
    small = pl.BlockSpec((1, tc, nv2, LANES), lambda i, j: (i, j, 0, 0))
    st = pl.BlockSpec((1, nv2, RWKV_N, LANES), lambda i, j: (i, 0, 0, 0))
    return pl.pallas_call(
        functools.partial(_rwkv_scan_body, tc=tc),
        grid=(g, t // tc),
        in_specs=[big] * 5 + [small, st],
        out_specs=[small, st],
        out_shape=[jax.ShapeDtypeStruct((g, t, nv2, LANES), F32),
                   jax.ShapeDtypeStruct((g, nv2, RWKV_N, LANES), F32)],
        scratch_shapes=[pltpu.VMEM((nv2, RWKV_N, LANES), F32)],
        compiler_params=_params(("parallel", "arbitrary")),
        name="rwkv_scan",
    )(wt, kkt, kkat, kt, rt, vt, s0)


def _rwkv_post_body(y_ref, r_ref, k_ref, v_ref, g_ref, rk_ref, lnw_ref, lnb_ref, bd_ref, o_ref):
    y = y_ref[...]
    bd = bd_ref[...]
    inv_n = 1.0 / RWKV_N
    mu = jnp.dot(y, bd, precision=HIGHEST, preferred_element_type=F32) * inv_n
    yc = y - mu
    var = jnp.dot(yc * yc, bd, precision=HIGHEST, preferred_element_type=F32) * inv_n
    yn = yc * lax.rsqrt(var + RWKV_GN_EPS) * lnw_ref[...] + lnb_ref[...]
    v = v_ref[...]
    bonus = jnp.dot(r_ref[...] * k_ref[...] * rk_ref[...], bd, precision=HIGHEST, preferred_element_type=F32) * v
    o_ref[...] = ((yn + bonus) * g_ref[...]).astype(o_ref.dtype)


def _rwkv_post(y, r, k, v, g, rk, lnw, lnb, bd, tm):
    m = y.shape[0]
    vec = _const_spec((1, RWKV_W))
    return pl.pallas_call(
        _rwkv_post_body,
        grid=(m // tm,),
        in_specs=[_row_spec(tm, RWKV_W)] * 5 + [vec, vec, vec, _const_spec(bd.shape)],
        out_specs=_row_spec(tm, RWKV_W),
        out_shape=jax.ShapeDtypeStruct((m, RWKV_W), BF16),
        compiler_params=_params(("parallel",)),
        name="rwkv_post",
    )(y, r, k, v, g, rk, lnw, lnb, bd)


def _to_chains(x, nb, t):
    bg = CHAINS // RWKV_HEADS
    g = nb // bg
    x = x.reshape(g, bg, t, RWKV_HEADS, RWKV_N).transpose(0, 2, 4, 1, 3).reshape(g, t, RWKV_N, CHAINS)
    return jnp.concatenate([x, x], axis=-1)


def _v_to_chains(x, nb, t):
    bg = CHAINS // RWKV_HEADS
    g = nb // bg
    x = x.reshape(g, bg, t, RWKV_HEADS, RWKV_N // 2, 2).transpose(0, 2, 4, 5, 1, 3)
    return x.reshape(g, t, RWKV_N // 2, LANES)


def _y_from_chains(y, nb, t):
    bg = CHAINS // RWKV_HEADS
    g = nb // bg
    y = y.reshape(g, t, RWKV_N // 2, 2, bg, RWKV_HEADS).transpose(0, 4, 1, 5, 2, 3)
    return y.reshape(nb * t, RWKV_W)


def _state_to_chains(s, nb):
    bg = CHAINS // RWKV_HEADS
    g = nb // bg
    s = s.reshape(g, bg, RWKV_HEADS, RWKV_N // 2, 2, RWKV_N).transpose(0, 3, 5, 4, 1, 2)
    return s.reshape(g, RWKV_N // 2, RWKV_N, LANES)


def _state_from_chains(s, nb):
    bg = CHAINS // RWKV_HEADS
    g = nb // bg
    s = s.reshape(g, RWKV_N // 2, RWKV_N, 2, bg, RWKV_HEADS).transpose(0, 4, 5, 1, 3, 2)
    return s.reshape(nb, RWKV_HEADS, RWKV_N, RWKV_N)


def _s5_body(u_ref, x0_ref, lre_ref, lim_ref, ldt_ref, bre_ref, bim_ref, cre_ref, cim_ref, d_ref, gw_ref, gbias_ref,
             o_ref, xf_ref, st_ref, ab_ref, bbar_ref, bu_ref, xs_ref, *, nrow, tc):
    j = pl.program_id(0)

    @pl.when(j == 0)
    def _():
        dt = jnp.exp(ldt_ref[...])
        lre = lre_ref[...]
        lim = lim_ref[...]
        mag = jnp.exp(lre * dt)
        are = mag * jnp.cos(lim * dt)
        aim = mag * jnp.sin(lim * dt)
        den = lre * lre + lim * lim
        nr = are - 1.0
        cre = (nr * lre + aim * lim) / den
        cim = (aim * lre - nr * lim) / den
        ab_ref[0:1, :] = are
        ab_ref[1:2, :] = aim
        bre = bre_ref[...]
        bim = bim_ref[...]
        bbar_ref[:, 0:S5_N] = (bre * cre - bim * cim).astype(BF16)
        bbar_ref[:, S5_N:2 * S5_N] = (bre * cim + bim * cre).astype(BF16)
        st_ref[...] = x0_ref[...]

    u = u_ref[...]
    bu_ref[...] = _dot(u.astype(BF16), bbar_ref[...])
    are = ab_ref[0:1, :]
    aim = ab_ref[1:2, :]

    def token(t, carry):
        rows = pl.ds(pl.multiple_of(t * nrow, nrow), nrow)
        xr = st_ref[:, 0:S5_N]
        xi = st_ref[:, S5_N:2 * S5_N]
        nr = are * xr - aim * xi + bu_ref[rows, 0:S5_N]
        ni = are * xi + aim * xr + bu_ref[rows, S5_N:2 * S5_N]
        st_ref[:, 0:S5_N] = nr
        st_ref[:, S5_N:2 * S5_N] = ni
        xs_ref[rows, 0:S5_N] = nr
        xs_ref[rows, S5_N:2 * S5_N] = ni
        return carry

    lax.fori_loop(0, tc, token, 0)

    y = (_dot(xs_ref[:, 0:S5_N].astype(BF16), cre_ref[...]) - _dot(xs_ref[:, S5_N:2 * S5_N].astype(BF16), cim_ref[...])
         + d_ref[...] * u)
    y = 0.5 * y * (1.0 + jnp.tanh(math.sqrt(2.0 / math.pi) * (y + 0.044715 * (y * y * y))))
    gate = _dot(y.astype(BF16), gw_ref[...]) + gbias_ref[...]
    o_ref[...] = (y * _sigmoid(gate)).astype(o_ref.dtype)

    @pl.when(j == pl.num_programs(0) - 1)
    def _():
        xf_ref[...] = st_ref[...]


def _s5(u, x0, lre, lim, ldt, bre, bim, cre, cim, d, gw, gbias, nrow, t):
    tc = _pick_block(t, 1, 48)
    rows = tc * nrow
    vec = _const_spec((1, S5_N))
    return pl.pallas_call(
        functools.partial(_s5_body, nrow=nrow, tc=tc),
        grid=(t // tc,),
        in_specs=[_row_spec(rows, S5_WIDTH), _const_spec((nrow, 2 * S5_N)), vec, vec, vec,
                  _const_spec(bre.shape), _const_spec(bim.shape), _const_spec(cre.shape), _const_spec(cim.shape),
                  _const_spec((1, S5_WIDTH)), _const_spec(gw.shape), _const_spec((1, S5_WIDTH))],
        out_specs=[_row_spec(rows, S5_WIDTH), pl.BlockSpec((nrow, 2 * S5_N), lambda i: (0, 0))],
        out_shape=[jax.ShapeDtypeStruct((t * nrow, S5_WIDTH), BF16), jax.ShapeDtypeStruct((nrow, 2 * S5_N), F32)],
        scratch_shapes=[pltpu.VMEM((nrow, 2 * S5_N), F32), pltpu.VMEM((8, S5_N), F32),
                        pltpu.VMEM((S5_WIDTH, 2 * S5_N), BF16), pltpu.VMEM((rows, 2 * S5_N), F32),
                        pltpu.VMEM((rows, 2 * S5_N), F32)],
        compiler_params=_params(("arbitrary",)),
        name="s5",
    )(u, x0, lre, lim, ldt, bre, bim, cre, cim, d, gw, gbias)


def _merge_body(x_ref, og_ref, or_ref, os_ref, gt_ref, wg_ref, wr_ref, ws_ref, wo_ref, o_ref, *, d):
    m = (_sigmoid(gt_ref[:, 0:d]) * _dot(og_ref[...], wg_ref[...])
         + _sigmoid(gt_ref[:, d:2 * d]) * _dot(or_ref[...], wr_ref[...])
         + _sigmoid(gt_ref[:, 2 * d:3 * d]) * _dot(os_ref[...], ws_ref[...]))
    o_ref[...] = x_ref[...] + _dot(m.astype(BF16), wo_ref[...])


def _merge(x, og, orw, os5, gates, wg, wr, ws, wo, tm):
    m, d = x.shape
    return pl.pallas_call(
        functools.partial(_merge_body, d=d),
        grid=(m // tm,),
        in_specs=[_row_spec(tm, d), _row_spec(tm, GLA_V), _row_spec(tm, RWKV_W), _row_spec(tm, S5_WIDTH),
                  _row_spec(tm, 3 * d)] + [_const_spec(w.shape) for w in (wg, wr, ws, wo)],
        out_specs=_row_spec(tm, d),
        out_shape=jax.ShapeDtypeStruct((m, d), F32),
        compiler_params=_params(("parallel",)),
        name="merge",
    )(x, og, orw, os5, gates, wg, wr, ws, wo)


def _ffn_body(x_ref, g_ref, w1_ref, w3_ref, w2_ref, o_ref, *, dff, cf):
    x = x_ref[...]
    h = _rms(x, g_ref[...]).astype(BF16)
    acc = x
    for c0 in range(0, dff, cf):
        a = _dot(h, w1_ref[:, c0:c0 + cf])
        b = _dot(h, w3_ref[:, c0:c0 + cf])
        acc = acc + _dot((a * _sigmoid(a) * b).astype(BF16), w2_ref[c0:c0 + cf, :])
    o_ref[...] = acc


def _ffn(x, g, w1, w3, w2, tm):
    m, d = x.shape
    dff = w1.shape[1]
    cf = _pick_block(dff, LANES, 512)
    return pl.pallas_call(
        functools.partial(_ffn_body, dff=dff, cf=cf),
        grid=(m // tm,),
        in_specs=[_row_spec(tm, d), _const_spec((1, d)), _const_spec(w1.shape), _const_spec(w3.shape),
                  _const_spec(w2.shape)],
        out_specs=_row_spec(tm, d),
        out_shape=jax.ShapeDtypeStruct((m, d), F32),
        compiler_params=_params(("parallel",)),
        name="ffn",
    )(x, g, w1, w3, w2)


def _final_norm_body(x_ref, g_ref, o_ref):
    o_ref[...] = _rms(x_ref[...], g_ref[...])


def _final_norm(x, g, tm):
    m, d = x.shape
    return pl.pallas_call(
        _final_norm_body,
        grid=(m // tm,),
        in_specs=[_row_spec(tm, d), _const_spec((1, d))],
        out_specs=_row_spec(tm, d),
        out_shape=jax.ShapeDtypeStruct((m, d), F32),
        compiler_params=_params(("parallel",)),
        name="final_norm",
    )(x, g)


def _block_ones(n, blk, dtype):
    i = jnp.arange(n) // blk
    return (i[:, None] == i[None, :]).astype(dtype)


def _gla_consts():
    c = GLA_SUB
    tri = (jnp.arange(c)[:, None] >= jnp.arange(c)[None, :]).astype(F32)
    segb = (jnp.arange(GLA_QK)[:, None] // GLA_DK == jnp.arange(GLA_V)[None, :] // GLA_DV).astype(BF16)
    ones = jnp.ones((c, GLA_DV), BF16)
    sel = (jnp.arange(16)[:, None] // 3 == jnp.arange(3 * GLA_DV)[None, :] // GLA_DV).astype(BF16)
    return tri, segb, ones, sel


def _block_diag_expand(p):
    g, a, b = p.shape
    eye = jnp.eye(g, dtype=p.dtype)
    return (p[:, :, None, :] * eye[:, None, :, None]).reshape(g * a, g * b)


def kernel(x_prompt, x_sample, state_gla, state_rwkv, state_rwkv_shift, state_s5_re, state_s5_im, meta_tokens, norm_mix, norm_ffn, w_in, gla_gate_w2, gla_gate_b, gla_norm, rwkv_mu, rwkv_w0, rwkv_w2, rwkv_a0, rwkv_a2, rwkv_g2, rwkv_k_k, rwkv_k_a, rwkv_r_k, rwkv_ln_w, rwkv_ln_b, s5_a_re, s5_a_im, s5_log_dt, s5_b_re, s5_b_im, s5_c_re, s5_c_im, s5_d, s5_glu_w, s5_glu_b, w_br_gla, w_br_rwkv, w_br_s5, w_out, ffn_w1, ffn_w3, ffn_w2, final_norm):
    nb, seq, d = x_prompt.shape
    ns = x_sample.shape[0]
    assert x_sample.shape[1] == 1
    depth = w_in.shape[0]
    t = seq + N_META
    mp = nb * t
    m = mp + ns
    assert (nb * RWKV_HEADS) % CHAINS == 0 and (ns * RWKV_HEADS) % CHAINS == 0
    tm = _pick_block(m, 8, 256)

    meta = jnp.broadcast_to(meta_tokens.astype(x_prompt.dtype)[None], (nb, N_META, d))
    x = jnp.concatenate([jnp.concatenate([meta, x_prompt], axis=1).reshape(mp, d), x_sample.reshape(ns, d)], axis=0)

    tri, segb, ones, sel = _gla_consts()
    bd = _block_ones(RWKV_W, RWKV_N, F32)
    o_q, o_k, o_v, o_g, o_z = 0, GLA_QK, 2 * GLA_QK, 2 * GLA_QK + GLA_V, 2 * GLA_QK + 2 * GLA_V
    o_r = o_z + GLA_GATE_RANK
    o_s = o_r + RWKV_COLS
    o_t = o_s + S5_WIDTH

    new_p = [[] for _ in range(5)]
    new_s = [[] for _ in range(5)]
    for l in range(depth):
        wl = w_in[l]
        wg = jnp.concatenate([wl[:, o_q:o_r], jnp.zeros((d, GLA_ZPAD - GLA_GATE_RANK), F32)], axis=1).astype(BF16)
        wr = wl[:, o_r:o_s].astype(BF16)
        ws = wl[:, o_s:o_t].astype(BF16)
        wt = wl[:, o_t:].astype(BF16)
        pg, pr, ps, gates = _inproj(x, norm_mix[l][None], wg, wr, ws, wt, tm)

        w2p = jnp.concatenate([gla_gate_w2[l], jnp.zeros((GLA_ZPAD - GLA_GATE_RANK, GLA_QK), F32)], axis=0).astype(BF16)
        gb = gla_gate_b[l][None]
        gn = gla_norm[l][None]
        og_p, gla_p = _gla_prompt(pg, w2p, gb, gn, (tri, segb, ones), nb, t)
        og_s, gla_s = _gla_sample(pg, state_gla[l].reshape(ns, GLA_QK, GLA_DV), w2p, gb, gn, sel, mp, ns)
        og = jnp.concatenate([og_p, og_s], axis=0)
        new_p[0].append(gla_p.reshape(nb, GLA_HEADS, GLA_DK, GLA_DV))
        new_s[0].append(gla_s.reshape(ns, GLA_HEADS, GLA_DK, GLA_DV))

        prp = pr[:mp].reshape(nb, t, RWKV_COLS)
        prev = jnp.concatenate([jnp.zeros((nb, 1, RWKV_COLS), F32), prp[:, :-1]], axis=1).reshape(mp, RWKV_COLS)
        prev = jnp.concatenate([prev, state_rwkv_shift[l]], axis=0)
        wlow = jnp.zeros((RWKV_LOW, 3 * RWKV_W), F32)
        wlow = wlow.at[0:RWKV_W_RANK, 0:RWKV_W].set(rwkv_w2[l])
        wlow = wlow.at[RWKV_W_RANK:RWKV_W_RANK + RWKV_A_RANK, RWKV_W:2 * RWKV_W].set(rwkv_a2[l])
        wlow = wlow.at[RWKV_W_RANK + RWKV_A_RANK:, 2 * RWKV_W:].set(rwkv_g2[l])
        r_, w_, k_, v_, kk_, kka_, g_ = _rwkv_prep(
            pr, prev, rwkv_mu[l][None], rwkv_w0[l][None], rwkv_a0[l][None], rwkv_k_k[l][None], rwkv_k_a[l][None],
            wlow.astype(BF16), bd, tm)
        y_parts, st_parts = [], []
        for lo, n_b, n_t, s0 in ((0, nb, t, jnp.zeros((nb, RWKV_HEADS, RWKV_N, RWKV_N), F32)),
                                 (mp, ns, 1, state_rwkv[l])):
            sl = slice(lo, lo + n_b * n_t)
            yt, sf = _rwkv_scan(*[_to_chains(a[sl], n_b, n_t) for a in (w_, kk_, kka_, k_, r_)],
                                _v_to_chains(v_[sl], n_b, n_t), _state_to_chains(s0, n_b))
            y_parts.append(_y_from_chains(yt, n_b, n_t))
            st_parts.append(_state_from_chains(sf, n_b))
        orw = _rwkv_post(jnp.concatenate(y_parts, axis=0), r_, k_, v_, g_, rwkv_r_k[l].reshape(1, RWKV_W),
                         rwkv_ln_w[l][None], rwkv_ln_b[l][None], bd, tm)
        new_p[1].append(st_parts[0])
        new_s[1].append(st_parts[1])
        new_p[2].append(prp[:, -1])
        new_s[2].append(pr[mp:])

        lre = s5_a_re[l].reshape(1, S5_N)
        lim = s5_a_im[l].reshape(1, S5_N)
        ldt = jnp.broadcast_to(s5_log_dt[l][:, None], (S5_GROUPS, S5_STATE)).reshape(1, S5_N)
        bre = _block_diag_expand(jnp.swapaxes(s5_b_re[l], 1, 2))
        bim = _block_diag_expand(jnp.swapaxes(s5_b_im[l], 1, 2))
        cre = _block_diag_expand(jnp.swapaxes(s5_c_re[l], 1, 2)).astype(BF16)
        cim = _block_diag_expand(jnp.swapaxes(s5_c_im[l], 1, 2)).astype(BF16)
        s5_args = (lre, lim, ldt, bre, bim, cre, cim, s5_d[l][None], s5_glu_w[l].astype(BF16), s5_glu_b[l][None])
        u_p = ps[:mp].reshape(nb, t, S5_WIDTH).transpose(1, 0, 2).reshape(mp, S5_WIDTH)
        os_p, xf_p = _s5(u_p, jnp.zeros((nb, 2 * S5_N), F32), *s5_args, nb, t)
        os_p = os_p.reshape(t, nb, S5_WIDTH).transpose(1, 0, 2).reshape(mp, S5_WIDTH)
        x0_s = jnp.concatenate([state_s5_re[l].reshape(ns, S5_N), state_s5_im[l].reshape(ns, S5_N)], axis=1)
        os_s, xf_s = _s5(ps[mp:], x0_s, *s5_args, ns, 1)
        os5 = jnp.concatenate([os_p, os_s], axis=0)
        new_p[3].append(xf_p[:, :S5_N].reshape(nb, S5_GROUPS, S5_STATE))
        new_p[4].append(xf_p[:, S5_N:].reshape(nb, S5_GROUPS, S5_STATE))
        new_s[3].append(xf_s[:, :S5_N].reshape(ns, S5_GROUPS, S5_STATE))
        new_s[4].append(xf_s[:, S5_N:].reshape(ns, S5_GROUPS, S5_STATE))

        x = _merge(x, og, orw, os5, gates, w_br_gla[l].astype(BF16), w_br_rwkv[l].astype(BF16),
                   w_br_s5[l].astype(BF16), w_out[l].astype(BF16), tm)
        x = _ffn(x, norm_ffn[l][None], ffn_w1[l].astype(BF16), ffn_w3[l].astype(BF16), ffn_w2[l].astype(BF16), tm)

    y = _final_norm(x, final_norm[None], tm)
    y_prompt = y[:mp].reshape(nb, t, d)[:, N_META:]
    y_sample = y[mp:].reshape(ns, 1, d)
    outs_p = [jnp.stack(a, axis=0) for a in new_p]
    outs_s = [jnp.stack(a, axis=0) for a in new_s]
    return (y_prompt, y_sample, *outs_p, *outs_s)
```

```python
import functools
import math

import jax
import jax.numpy as jnp
from jax import lax
from jax.experimental import pallas as pl
from jax.experimental.pallas import tpu as pltpu

F32 = jnp.float32
BF16 = jnp.bfloat16
HIGHEST = lax.Precision.HIGHEST

N_META = 16
NORM_EPS = 1e-6
GLA_HEADS = 4
GLA_DK = 64
GLA_DV = 128
GLA_QK = GLA_HEADS * GLA_DK
GLA_V = GLA_HEADS * GLA_DV
GLA_GATE_RANK = 16
GLA_TAU = 16.0
GLA_SUB = 16
RWKV_HEADS = 8
RWKV_N = 64
RWKV_W = RWKV_HEADS * RWKV_N
RWKV_W_RANK = 64
RWKV_A_RANK = 64
RWKV_G_RANK = 128
RWKV_LOW = RWKV_W_RANK + RWKV_A_RANK + RWKV_G_RANK
RWKV_COLS = 3 * RWKV_W + RWKV_LOW
RWKV_GN_EPS = 64e-5
S5_GROUP = 16
S5_GROUPS = 32
S5_WIDTH = S5_GROUPS * S5_GROUP
S5_STATE = 64
S5_N = S5_GROUPS * S5_STATE

LANES = 128
CHAINS = LANES // 2
GLA_ZPAD = LANES
GLA_COLS = 2 * GLA_QK + 2 * GLA_V + GLA_ZPAD
VMEM_LIMIT = 56 << 20


def _params(sem):
    return pltpu.CompilerParams(dimension_semantics=sem, vmem_limit_bytes=VMEM_LIMIT)


def _const_spec(shape):
    nd = len(shape)
    return pl.BlockSpec(shape, lambda *_: (0,) * nd, pipeline_mode=pl.Buffered(1))


def _row_spec(tm, cols, off=0):
    return pl.BlockSpec((tm, cols), lambda i: (i + off, 0))


def _pick_block(n, mult, cap):
    best = None
    for d in range(mult, min(n, cap) + 1, mult):
        if n % d == 0:
            best = d
    assert best is not None, (n, mult, cap)
    return best


def _sigmoid(x):
    return 1.0 / (1.0 + jnp.exp(-x))


def _softplus(x):
    return jnp.maximum(x, 0.0) + jnp.log1p(jnp.exp(-jnp.abs(x)))


def _rms(x, g):
    return x * lax.rsqrt(jnp.mean(x * x, axis=-1, keepdims=True) + NORM_EPS) * g


def _dot(a, b):
    return jnp.dot(a, b, preferred_element_type=F32)


def _dot_t0(a, b):
    return lax.dot_general(a, b, (((0,), (0,)), ((), ())), preferred_element_type=F32)


def _split3_rows(x):
    hi = x.astype(BF16).astype(F32)
    r1 = x - hi
    mid = r1.astype(BF16).astype(F32)
    lo = (r1 - mid).astype(BF16).astype(F32)
    return [hi, mid, lo]


def _inproj_body(x_ref, g_ref, wg_ref, wr_ref, ws_ref, wt_ref, og_ref, or_ref, os_ref, ot_ref):
    h = _rms(x_ref[...], g_ref[...]).astype(BF16)
    og_ref[...] = _dot(h, wg_ref[...])
    or_ref[...] = _dot(h, wr_ref[...])
    os_ref[...] = _dot(h, ws_ref[...])
    ot_ref[...] = _dot(h, wt_ref[...])


def _inproj(x, g, wg, wr, ws, wt, tm):
    m, d = x.shape
    widths = (wg.shape[1], wr.shape[1], ws.shape[1], wt.shape[1])
    return pl.pallas_call(
        _inproj_body,
        grid=(m // tm,),
        in_specs=[_row_spec(tm, d), _const_spec((1, d))] + [_const_spec(w.shape) for w in (wg, wr, ws, wt)],
        out_specs=[_row_spec(tm, c) for c in widths],
        out_shape=[jax.ShapeDtypeStruct((m, c), F32) for c in widths],
        compiler_params=_params(("parallel",)),
        name="inproj",
    )(x, g, wg, wr, ws, wt)


def _log_sigmoid(x):
    return jnp.minimum(x, 0.0) - jnp.log1p(jnp.exp(-jnp.abs(x)))


def _gla_finish(o, g, gn):
    parts = []
    for h in range(GLA_HEADS):
        oh = o[:, h * GLA_DV:(h + 1) * GLA_DV]
        parts.append(oh * lax.rsqrt(jnp.mean(oh * oh, axis=-1, keepdims=True) + NORM_EPS))
    on = jnp.concatenate(parts, axis=1)
    return on * gn * (g * _sigmoid(g))


def _gla_log_decay(z, w2_ref, gb_ref):
    zz = _dot(z.astype(BF16), w2_ref[...]) + gb_ref[...]
    return _log_sigmoid(zz) * (1.0 / GLA_TAU)


def _gla_chunk(qs, k, v, bc, s, segb, ones):
    c = GLA_SUB
    lane_head = lax.broadcasted_iota(jnp.int32, (c, GLA_QK), 1) // GLA_DK
    row_id = lax.broadcasted_iota(jnp.int32, (c, GLA_QK), 0)
    bl = bc[c - 1:c, :]
    qe = qs * jnp.exp(bc)
    lhs = jnp.concatenate([jnp.where(lane_head == h, qe, 0.0) for h in range(GLA_HEADS)], axis=0)
    oi = _dot(lhs.astype(BF16), s.astype(BF16))
    o_inter = jnp.concatenate([oi[h * c:(h + 1) * c] for h in range(GLA_HEADS)], axis=1)
    terms = []
    for jj in range(c):
        e = jnp.exp(jnp.minimum(bc - bc[jj:jj + 1, :], 0.0))
        terms.append(jnp.where(row_id >= jj, qs * k[jj:jj + 1, :] * e, 0.0))
    pw = _dot(jnp.concatenate(terms, axis=0).astype(BF16), segb)
    o_intra = pw[0:c] * v[0:1, :]
    for jj in range(1, c):
        o_intra = o_intra + pw[jj * c:(jj + 1) * c] * v[jj:jj + 1, :]
    kd = k * jnp.exp(bl - bc)
    upd = _dot_t0(kd.astype(BF16), v.astype(BF16))
    kv = jnp.concatenate(
        [upd[h * GLA_DK:(h + 1) * GLA_DK, h * GLA_DV:(h + 1) * GLA_DV] for h in range(GLA_HEADS)], axis=0)
    ef = jnp.concatenate(_split3_rows(jnp.exp(bl)) + [jnp.zeros((c - 3, GLA_QK), F32)], axis=0)
    dec = _dot_t0(ef.astype(BF16), ones)
    return o_inter + o_intra, dec * s + kv


def _gla_prompt_body(p_ref, w2_ref, gb_ref, gn_ref, tri_ref, segb_ref, ones_ref, o_ref, sf_ref, s_ref, *,
                     ngroup, unroll):
    j = pl.program_id(1)
    c = GLA_SUB
    gr = c * unroll

    @pl.when(j == 0)
    def _():
        s_ref[...] = jnp.zeros_like(s_ref)

    def group(gi, carry):
        rows = pl.ds(pl.multiple_of(gi * gr, gr), gr)
        q = p_ref[rows, 0:GLA_QK]
        k = p_ref[rows, GLA_QK:2 * GLA_QK]
        v = p_ref[rows, 2 * GLA_QK:2 * GLA_QK + GLA_V]
        g = p_ref[rows, 2 * GLA_QK + GLA_V:2 * GLA_QK + 2 * GLA_V]
        z = p_ref[rows, 2 * GLA_QK + 2 * GLA_V:GLA_COLS]
        la = _gla_log_decay(z, w2_ref, gb_ref)
        bc = jnp.dot(tri_ref[...], la, precision=HIGHEST, preferred_element_type=F32)
        qs = q * (GLA_DK ** -0.5)
        s = s_ref[...]
        outs = []
        for u in range(unroll):
            sl = slice(u * c, (u + 1) * c)
            o_u, s = _gla_chunk(qs[sl], k[sl], v[sl], bc[sl], s, segb_ref[...], ones_ref[...])
            outs.append(o_u)
        o = outs[0] if unroll == 1 else jnp.concatenate(outs, axis=0)
        o_ref[rows, :] = _gla_finish(o, g, gn_ref[...]).astype(o_ref.dtype)
        s_ref[...] = s
        return carry

    lax.fori_loop(0, ngroup, group, 0)

    @pl.when(j == pl.num_programs(1) - 1)
    def _():
        sf_ref[0] = s_ref[...]


def _gla_prompt(pg, w2p, gb, gn, consts, nb, t, unroll):
    gr = GLA_SUB * unroll
    tb = _pick_block(t, gr, 2304)
    nt = t // tb
    tri, segb, ones = consts
    return pl.pallas_call(
        functools.partial(_gla_prompt_body, ngroup=tb // gr, unroll=unroll),
        grid=(nb, nt),
        in_specs=[pl.BlockSpec((tb, GLA_COLS), lambda b, j: (b * nt + j, 0)),
                  _const_spec(w2p.shape), _const_spec(gb.shape), _const_spec(gn.shape),
                  _const_spec(tri.shape), _const_spec(segb.shape), _const_spec(ones.shape)],
        out_specs=[pl.BlockSpec((tb, GLA_V), lambda b, j: (b * nt + j, 0)),
                   pl.BlockSpec((1, GLA_QK, GLA_DV), lambda b, j: (b, 0, 0))],
        out_shape=[jax.ShapeDtypeStruct((nb * t, GLA_V), BF16),
                   jax.ShapeDtypeStruct((nb, GLA_QK, GLA_DV), F32)],
        scratch_shapes=[pltpu.VMEM((GLA_QK, GLA_DV), F32)],
        compiler_params=_params(("parallel", "arbitrary")),
        name="gla_prompt",
    )(pg, w2p, gb, gn, tri, segb, ones)


def _gla_sample_body(p_ref, s0_ref, w2_ref, gb_ref, gn_ref, sel_ref, o_ref, s_ref, *, nrow):
    q = p_ref[:, 0:GLA_QK]
    k = p_ref[:, GLA_QK:2 * GLA_QK]
    v = p_ref[:, 2 * GLA_QK:2 * GLA_QK + GLA_V]
    g = p_ref[:, 2 * GLA_QK + GLA_V:2 * GLA_QK + 2 * GLA_V]
    z = p_ref[:, 2 * GLA_QK + 2 * GLA_V:GLA_COLS]
    a = jnp.exp(_gla_log_decay(z, w2_ref, gb_ref))
    qs = q * (GLA_DK ** -0.5)
    out_rows = []
    for i in range(nrow):
        pieces = _split3_rows(a[i:i + 1]) + _split3_rows(k[i:i + 1]) + _split3_rows(qs[i:i + 1])
        ef = jnp.concatenate(pieces + [jnp.zeros((16 - 9, GLA_QK), F32)], axis=0)
        cb = _dot_t0(ef.astype(BF16), sel_ref[...])
        ab = cb[:, 0:GLA_DV]
        kb = cb[:, GLA_DV:2 * GLA_DV]
        qb = cb[:, 2 * GLA_DV:3 * GLA_DV]
        vb = jnp.concatenate(
            [jnp.broadcast_to(v[i:i + 1, h * GLA_DV:(h + 1) * GLA_DV], (GLA_DK, GLA_DV)) for h in range(GLA_HEADS)],
            axis=0)
        s = ab * s0_ref[i] + kb * vb
        s_ref[i] = s
        op = qb * s
        out_rows.append(jnp.concatenate(
            [jnp.sum(op[h * GLA_DK:(h + 1) * GLA_DK], axis=0, keepdims=True) for h in range(GLA_HEADS)], axis=1))
    o = jnp.concatenate(out_rows, axis=0)
    o_ref[...] = _gla_finish(o, g, gn_ref[...]).astype(o_ref.dtype)


def _gla_sample(pg, s0, w2p, gb, gn, sel, row0, ns):
    nrow = 16
    assert ns % nrow == 0 and row0 % nrow == 0
    return pl.pallas_call(
        functools.partial(_gla_sample_body, nrow=nrow),
        grid=(ns // nrow,),
        in_specs=[_row_spec(nrow, GLA_COLS, row0 // nrow),
                  pl.BlockSpec((nrow, GLA_QK, GLA_DV), lambda i: (i, 0, 0)),
                  _const_spec(w2p.shape), _const_spec(gb.shape), _const_spec(gn.shape), _const_spec(sel.shape)],
        out_specs=[_row_spec(nrow, GLA_V), pl.BlockSpec((nrow, GLA_QK, GLA_DV), lambda i: (i, 0, 0))],
        out_shape=[jax.ShapeDtypeStruct((ns, GLA_V), BF16), jax.ShapeDtypeStruct((ns, GLA_QK, GLA_DV), F32)],
        compiler_params=_params(("parallel",)),
        name="gla_sample",
    )(pg, s0, w2p, gb, gn, sel)


def _segsum(x, bd):
    hi = x.astype(BF16)
    lo = (x - hi.astype(F32)).astype(BF16)
    return _dot(hi, bd) + _dot(lo, bd)


def _rwkv_prep_math(p, prev, mu_ref, w0_ref, a0_ref, kkw_ref, kaw_ref, wlow_ref, bd_ref,
                    r_o, w_o, k_o, v_o, kk_o, kka_o, g_o):
    xs = p + (prev - p) * mu_ref[...]
    r = xs[:, 0:RWKV_W]
    k = xs[:, RWKV_W:2 * RWKV_W]
    v = xs[:, 2 * RWKV_W:3 * RWKV_W]
    zl = xs[:, 3 * RWKV_W:RWKV_COLS]
    lane = lax.broadcasted_iota(jnp.int32, zl.shape, 1)
    zt = jnp.where(lane < RWKV_W_RANK, jnp.tanh(zl),
                   jnp.where(lane < RWKV_W_RANK + RWKV_A_RANK, zl, _sigmoid(zl)))
    low = _dot(zt.astype(BF16), wlow_ref[...])
    w_log = -_softplus(-(w0_ref[...] + low[:, 0:RWKV_W])) - 0.5
    a = _sigmoid(a0_ref[...] + low[:, RWKV_W:2 * RWKV_W])
    kk = k * kkw_ref[...]
    kk = kk * lax.rsqrt(_segsum(kk * kk, bd_ref[...]) + 1e-12)
    r_o[...] = r
    w_o[...] = jnp.exp(-jnp.exp(w_log))
    k_o[...] = k * (1.0 + (a - 1.0) * kaw_ref[...])
    v_o[...] = v
    kk_o[...] = kk
    kka_o[...] = kk * a
    g_o[...] = low[:, 2 * RWKV_W:3 * RWKV_W]


def _rwkv_prep_seq_body(p_ref, *rest):
    carry_ref = rest[-1]

    @pl.when(pl.program_id(1) == 0)
    def _():
        carry_ref[...] = jnp.zeros_like(carry_ref)

    p = p_ref[...]
    row = lax.broadcasted_iota(jnp.int32, p.shape, 0)
    prev = jnp.where(row == 0, carry_ref[...], pltpu.roll(p, 1, axis=0))
    carry_ref[...] = p[p.shape[0] - 1:, :]
    _rwkv_prep_math(p, prev, *rest[:-1])


def _rwkv_prep_step_body(p_ref, pp_ref, *rest):
    _rwkv_prep_math(p_ref[...], pp_ref[...], *rest)


def _rwkv_prep(p, prev, consts, nb, t, row0):
    vec = _const_spec((1, RWKV_W))
    mu, w0, a0, kkw, kaw, wlow, bd = consts
    const_specs = [_const_spec((1, RWKV_COLS)), vec, vec, vec, vec, _const_spec(wlow.shape), _const_spec(bd.shape)]
    if prev is None:
        tb = _pick_block(t, 8, 768)
        nt = t // tb
        rows = pl.BlockSpec((tb, RWKV_COLS), lambda b, j: (b * nt + j, 0))
        outs = pl.BlockSpec((tb, RWKV_W), lambda b, j: (b * nt + j, 0))
        return pl.pallas_call(
            _rwkv_prep_seq_body,
            grid=(nb, nt),
            in_specs=[rows] + const_specs,
            out_specs=[outs] * 7,
            out_shape=[jax.ShapeDtypeStruct((nb * t, RWKV_W), F32)] * 7,
            scratch_shapes=[pltpu.VMEM((1, RWKV_COLS), F32)],
            compiler_params=_params(("parallel", "arbitrary")),
            name="rwkv_prep_seq",
        )(p, *consts)
    tm = _pick_block(nb, 8, 256)
    assert row0 % tm == 0
    return pl.pallas_call(
        _rwkv_prep_step_body,
        grid=(nb // tm,),
        in_specs=[_row_spec(tm, RWKV_COLS, row0 // tm), _row_spec(tm, RWKV_COLS)] + const_specs,
        out_specs=[_row_spec(tm, RWKV_W)] * 7,
        out_shape=[jax.ShapeDtypeStruct((nb, RWKV_W), F32)] * 7,
        compiler_params=_params(("parallel",)),
        name="rwkv_prep_step",
    )(p, prev, *consts)


def _rwkv_scan_body(wk_ref, bk_ref, r_ref, v_ref, s0_ref, y_ref, sf_ref, s_ref, *, tc):
    j = pl.program_id(1)
    nv2 = RWKV_N // 2

    @pl.when(j == 0)
    def _():
        s_ref[...] = s0_ref[0]

    low_half = lax.broadcasted_iota(jnp.int32, (RWKV_N, LANES), 1) < CHAINS

    def unpack(x):
        sw = pltpu.roll(x, CHAINS, axis=1)
        return jnp.where(low_half, x, sw), jnp.where(low_half, sw, x)

    def token(t, carry):
        w, kk = unpack(wk_ref[0, t])
        kka, k = unpack(bk_ref[0, t])
        r = r_ref[0, t]
        for v2 in range(nv2):
            s = s_ref[v2]
            sk = jnp.sum(s * kk, axis=0, keepdims=True)
            vv = v_ref[0, t, v2:v2 + 1, :]
            sn = s * w - sk * kka + vv * k
            s_ref[v2] = sn
            y_ref[0, t, v2:v2 + 1, :] = jnp.sum(sn * r, axis=0, keepdims=True)
        return carry

    lax.fori_loop(0, tc, token, 0)

    @pl.when(j == pl.num_programs(1) - 1)
    def _():
        sf_ref[0] = s_ref[...]


def _rwkv_scan(wk, bk, rr, vt, s0):
    g, t = wk.shape[0], wk.shape[1]
    tc = _pick_block(t, 1, 48)
    nv2 = RWKV_N // 2
    big = pl.BlockSpec((1, tc, RWKV_N, LANES), lambda i, j: (i, j, 0, 0))
    small = pl.BlockSpec((1, tc, nv2, LANES), lambda i, j: (i, j, 0, 0))
    st = pl.BlockSpec((1, nv2, RWKV_N, LANES), lambda i, j: (i, 0, 0, 0))
    return pl.pallas_call(
        functools.partial(_rwkv_scan_body, tc=tc),
        grid=(g, t // tc),
        in_specs=[big] * 3 + [small, st],
        out_specs=[small, st],
        out_shape=[jax.ShapeDtypeStruct((g, t, nv2, LANES), F32),
                   jax.ShapeDtypeStruct((g, nv2, RWKV_N, LANES), F32)],
        scratch_shapes=[pltpu.VMEM((nv2, RWKV_N, LANES), F32)],
        compiler_params=_params(("parallel", "arbitrary")),
        name="rwkv_scan",
    )(wk, bk, rr, vt, s0)


def _rwkv_post_body(y_ref, r_ref, k_ref, v_ref, g_ref, rk_ref, lnw_ref, lnb_ref, bd_ref, o_ref):
    y = y_ref[...]
    bd = bd_ref[...]
    inv_n = 1.0 / RWKV_N
    yc = y - _segsum(y, bd) * inv_n
    var = _segsum(yc * yc, bd) * inv_n
    yn = yc * lax.rsqrt(var + RWKV_GN_EPS) * lnw_ref[...] + lnb_ref[...]
    bonus = _segsum(r_ref[...] * k_ref[...] * rk_ref[...], bd) * v_ref[...]
    o_ref[...] = ((yn + bonus) * g_ref[...]).astype(o_ref.dtype)


def _rwkv_post(y, r, k, v, g, rk, lnw, lnb, bd):
    m = y.shape[0]
    tm = _pick_block(m, 16, 512)
    vec = _const_spec((1, RWKV_W))
    return pl.pallas_call(
        _rwkv_post_body,
        grid=(m // tm,),
        in_specs=[_row_spec(tm, RWKV_W)] * 5 + [vec, vec, vec, _const_spec(bd.shape)],
        out_specs=_row_spec(tm, RWKV_W),
        out_shape=jax.ShapeDtypeStruct((m, RWKV_W), BF16),
        compiler_params=_params(("parallel",)),
        name="rwkv_post",
    )(y, r, k, v, g, rk, lnw, lnb, bd)


def _to_chains(x, y, nb, t):
    bg = CHAINS // RWKV_HEADS
    g = nb // bg

    def one(a):
        return a.reshape(g, bg, t, RWKV_HEADS, RWKV_N).transpose(0, 2, 4, 1, 3).reshape(g, t, RWKV_N, CHAINS)

    return jnp.concatenate([one(x), one(y)], axis=-1)


def _v_to_chains(x, nb, t):
    bg = CHAINS // RWKV_HEADS
    g = nb // bg
    x = x.reshape(g, bg, t, RWKV_HEADS, RWKV_N // 2, 2).transpose(0, 2, 4, 5, 1, 3)
    return x.reshape(g, t, RWKV_N // 2, LANES)


def _y_from_chains(y, nb, t):
    bg = CHAINS // RWKV_HEADS
    g = nb // bg
    y = y.reshape(g, t, RWKV_N // 2, 2, bg, RWKV_HEADS).transpose(0, 4, 1, 5, 2, 3)
    return y.reshape(nb * t, RWKV_W)


def _state_to_chains(s, nb):
    bg = CHAINS // RWKV_HEADS
    g = nb // bg
    s = s.reshape(g, bg, RWKV_HEADS, RWKV_N // 2, 2, RWKV_N).transpose(0, 3, 5, 4, 1, 2)
    return s.reshape(g, RWKV_N // 2, RWKV_N, LANES)


def _state_from_chains(s, nb):
    bg = CHAINS // RWKV_HEADS
    g = nb // bg
    s = s.reshape(g, RWKV_N // 2, RWKV_N, 2, bg, RWKV_HEADS).transpose(0, 4, 5, 1, 3, 2)
    return s.reshape(nb, RWKV_HEADS, RWKV_N, RWKV_N)


def _s5_body(u_ref, x0_ref, lre_ref, lim_ref, ldt_ref, bre_ref, bim_ref, cre_ref, cim_ref, d_ref, gw_ref, gbias_ref,
             o_ref, xf_ref, st_ref, ab_ref, bbar_ref, bu_ref, xs_ref, *, nrow, tc):
    j = pl.program_id(0)

    @pl.when(j == 0)
    def _():
        dt = jnp.exp(ldt_ref[...])
        lre = lre_ref[...]
        lim = lim_ref[...]
        mag = jnp.exp(lre * dt)
        are = mag * jnp.cos(lim * dt)
        aim = mag * jnp.sin(lim * dt)
        den = lre * lre + lim * lim
        nr = are - 1.0
        cre = (nr * lre + aim * lim) / den
        cim = (aim * lre - nr * lim) / den
        ab_ref[0:1, :] = are
        ab_ref[1:2, :] = aim
        bre = bre_ref[...]
        bim = bim_ref[...]
        bbar_ref[:, 0:S5_N] = (bre * cre - bim * cim).astype(BF16)
        bbar_ref[:, S5_N:2 * S5_N] = (bre * cim + bim * cre).astype(BF16)
        st_ref[...] = x0_ref[...]

    u = u_ref[...]
    bu_ref[...] = _dot(u.astype(BF16), bbar_ref[...])
    are = ab_ref[0:1, :]
    aim = ab_ref[1:2, :]

    def token(t, carry):
        rows = pl.ds(pl.multiple_of(t * nrow, nrow), nrow)
        xr = st_ref[:, 0:S5_N]
        xi = st_ref[:, S5_N:2 * S5_N]
        nr = are * xr - aim * xi + bu_ref[rows, 0:S5_N]
        ni = are * xi + aim * xr + bu_ref[rows, S5_N:2 * S5_N]
        st_ref[:, 0:S5_N] = nr
        st_ref[:, S5_N:2 * S5_N] = ni
        xs_ref[rows, 0:S5_N] = nr
        xs_ref[rows, S5_N:2 * S5_N] = ni
        return carry

    lax.fori_loop(0, tc, token, 0)

    y = (_dot(xs_ref[:, 0:S5_N].astype(BF16), cre_ref[...]) - _dot(xs_ref[:, S5_N:2 * S5_N].astype(BF16), cim_ref[...])
         + d_ref[...] * u)
    y = 0.5 * y * (1.0 + jnp.tanh(math.sqrt(2.0 / math.pi) * (y + 0.044715 * (y * y * y))))
    gate = _dot(y.astype(BF16), gw_ref[...]) + gbias_ref[...]
    o_ref[...] = (y * _sigmoid(gate)).astype(o_ref.dtype)

    @pl.when(j == pl.num_programs(0) - 1)
    def _():
        xf_ref[...] = st_ref[...]


def _s5(u, x0, lre, lim, ldt, bre, bim, cre, cim, d, gw, gbias, nrow, t):
    tc = _pick_block(t, 1, 48)
    rows = tc * nrow
    vec = _const_spec((1, S5_N))
    return pl.pallas_call(
        functools.partial(_s5_body, nrow=nrow, tc=tc),
        grid=(t // tc,),
        in_specs=[_row_spec(rows, S5_WIDTH), _const_spec((nrow, 2 * S5_N)), vec, vec, vec,
                  _const_spec(bre.shape), _const_spec(bim.shape), _const_spec(cre.shape), _const_spec(cim.shape),
                  _const_spec((1, S5_WIDTH)), _const_spec(gw.shape), _const_spec((1, S5_WIDTH))],
        out_specs=[_row_spec(rows, S5_WIDTH), pl.BlockSpec((nrow, 2 * S5_N), lambda i: (0, 0))],
        out_shape=[jax.ShapeDtypeStruct((t * nrow, S5_WIDTH), BF16), jax.ShapeDtypeStruct((nrow, 2 * S5_N), F32)],
        scratch_shapes=[pltpu.VMEM((nrow, 2 * S5_N), F32), pltpu.VMEM((8, S5_N), F32),
                        pltpu.VMEM((S5_WIDTH, 2 * S5_N), BF16), pltpu.VMEM((rows, 2 * S5_N), F32),
                        pltpu.VMEM((rows, 2 * S5_N), F32)],
        compiler_params=_params(("arbitrary",)),
        name="s5",
    )(u, x0, lre, lim, ldt, bre, bim, cre, cim, d, gw, gbias)


def _merge_body(x_ref, og_ref, or_ref, os_ref, gt_ref, wg_ref, wr_ref, ws_ref, wo_ref, o_ref, *, d):
    m = (_sigmoid(gt_ref[:, 0:d]) * _dot(og_ref[...], wg_ref[...])
         + _sigmoid(gt_ref[:, d:2 * d]) * _dot(or_ref[...], wr_ref[...])
         + _sigmoid(gt_ref[:, 2 * d:3 * d]) * _dot(os_ref[...], ws_ref[...]))
    o_ref[...] = x_ref[...] + _dot(m.astype(BF16), wo_ref[...])


def _merge(x, og, orw, os5, gates, wg, wr, ws, wo, tm):
    m, d = x.shape
    return pl.pallas_call(
        functools.partial(_merge_body, d=d),
        grid=(m // tm,),
        in_specs=[_row_spec(tm, d), _row_spec(tm, GLA_V), _row_spec(tm, RWKV_W), _row_spec(tm, S5_WIDTH),
                  _row_spec(tm, 3 * d)] + [_const_spec(w.shape) for w in (wg, wr, ws, wo)],
        out_specs=_row_spec(tm, d),
        out_shape=jax.ShapeDtypeStruct((m, d), F32),
        compiler_params=_params(("parallel",)),
        name="merge",
    )(x, og, orw, os5, gates, wg, wr, ws, wo)


def _ffn_body(x_ref, g_ref, w1_ref, w3_ref, w2_ref, o_ref, *, dff, cf):
    x = x_ref[...]
    h = _rms(x, g_ref[...]).astype(BF16)
    acc = x
    for c0 in range(0, dff, cf):
        a = _dot(h, w1_ref[:, c0:c0 + cf])
        b = _dot(h, w3_ref[:, c0:c0 + cf])
        acc = acc + _dot((a * _sigmoid(a) * b).astype(BF16), w2_ref[c0:c0 + cf, :])
    o_ref[...] = acc


def _ffn(x, g, w1, w3, w2, tm):
    m, d = x.shape
    dff = w1.shape[1]
    cf = _pick_block(dff, LANES, 512)
    return pl.pallas_call(
        functools.partial(_ffn_body, dff=dff, cf=cf),
        grid=(m // tm,),
        in_specs=[_row_spec(tm, d), _const_spec((1, d)), _const_spec(w1.shape), _const_spec(w3.shape),
                  _const_spec(w2.shape)],
        out_specs=_row_spec(tm, d),
        out_shape=jax.ShapeDtypeStruct((m, d), F32),
        compiler_params=_params(("parallel",)),
        name="ffn",
    )(x, g, w1, w3, w2)


def _final_norm_body(x_ref, g_ref, o_ref):
    o_ref[...] = _rms(x_ref[...], g_ref[...])


def _final_norm(x, g, tm):
    m, d = x.shape
    return pl.pallas_call(
        _final_norm_body,
        grid=(m // tm,),
        in_specs=[_row_spec(tm, d), _const_spec((1, d))],
        out_specs=_row_spec(tm, d),
        out_shape=jax.ShapeDtypeStruct((m, d), F32),
        compiler_params=_params(("parallel",)),
        name="final_norm",
    )(x, g)


def _block_ones(n, blk, dtype):
    i = jnp.arange(n) // blk
    return (i[:, None] == i[None, :]).astype(dtype)


def _gla_consts(unroll):
    c = GLA_SUB
    i = jnp.arange(c * unroll)
    tri = ((i[:, None] >= i[None, :]) & (i[:, None] // c == i[None, :] // c)).astype(F32)
    segb = (jnp.arange(GLA_QK)[:, None] // GLA_DK == jnp.arange(GLA_V)[None, :] // GLA_DV).astype(BF16)
    ones = jnp.ones((c, GLA_DV), BF16)
    sel = (jnp.arange(16)[:, None] // 3 == jnp.arange(3 * GLA_DV)[None, :] // GLA_DV).astype(BF16)
    return tri, segb, ones, sel


def _block_diag_expand(p):
    g, a, b = p.shape
    eye = jnp.eye(g, dtype=p.dtype)
    return (p[:, :, None, :] * eye[:, None, :, None]).reshape(g * a, g * b)


def kernel(x_prompt, x_sample, state_gla, state_rwkv, state_rwkv_shift, state_s5_re, state_s5_im, meta_tokens, norm_mix, norm_ffn, w_in, gla_gate_w2, gla_gate_b, gla_norm, rwkv_mu, rwkv_w0, rwkv_w2, rwkv_a0, rwkv_a2, rwkv_g2, rwkv_k_k, rwkv_k_a, rwkv_r_k, rwkv_ln_w, rwkv_ln_b, s5_a_re, s5_a_im, s5_log_dt, s5_b_re, s5_b_im, s5_c_re, s5_c_im, s5_d, s5_glu_w, s5_glu_b, w_br_gla, w_br_rwkv, w_br_s5, w_out, ffn_w1, ffn_w3, ffn_w2, final_norm):
    nb, seq, d = x_prompt.shape
    ns = x_sample.shape[0]
    assert x_sample.shape[1] == 1
    depth = w_in.shape[0]
    t = seq + N_META
    mp = nb * t
    m = mp + ns
    assert (nb * RWKV_HEADS) % CHAINS == 0 and (ns * RWKV_HEADS) % CHAINS == 0
    tm = _pick_block(m, 8, 256)
    tm_big = _pick_block(m, 16, 640)

    meta = jnp.broadcast_to(meta_tokens.astype(x_prompt.dtype)[None], (nb, N_META, d))
    x = jnp.concatenate([jnp.concatenate([meta, x_prompt], axis=1).reshape(mp, d), x_sample.reshape(ns, d)], axis=0)

    gla_unroll = 3 if t % (3 * GLA_SUB) == 0 else 1
    tri, segb, ones, sel = _gla_consts(gla_unroll)
    bd = _block_ones(RWKV_W, RWKV_N, BF16)
    o_q, o_k, o_v, o_g, o_z = 0, GLA_QK, 2 * GLA_QK, 2 * GLA_QK + GLA_V, 2 * GLA_QK + 2 * GLA_V
    o_r = o_z + GLA_GATE_RANK
    o_s = o_r + RWKV_COLS
    o_t = o_s + S5_WIDTH

    new_p = [[] for _ in range(5)]
    new_s = [[] for _ in range(5)]
    for l in range(depth):
        wl = w_in[l]
        wg = jnp.concatenate([wl[:, o_q:o_r], jnp.zeros((d, GLA_ZPAD - GLA_GATE_RANK), F32)], axis=1).astype(BF16)
        wr = wl[:, o_r:o_s].astype(BF16)
        ws = wl[:, o_s:o_t].astype(BF16)
        wt = wl[:, o_t:].astype(BF16)
        pg, pr, ps, gates = _inproj(x, norm_mix[l][None], wg, wr, ws, wt, tm)

        w2p = jnp.concatenate([gla_gate_w2[l], jnp.zeros((GLA_ZPAD - GLA_GATE_RANK, GLA_QK), F32)], axis=0).astype(BF16)
        gb = gla_gate_b[l][None]
        gn = gla_norm[l][None]
        og_p, gla_p = _gla_prompt(pg, w2p, gb, gn, (tri, segb, ones), nb, t, gla_unroll)
        og_s, gla_s = _gla_sample(pg, state_gla[l].reshape(ns, GLA_QK, GLA_DV), w2p, gb, gn, sel, mp, ns)
        og = jnp.concatenate([og_p, og_s], axis=0)
        new_p[0].append(gla_p.reshape(nb, GLA_HEADS, GLA_DK, GLA_DV))
        new_s[0].append(gla_s.reshape(ns, GLA_HEADS, GLA_DK, GLA_DV))

        wlow = jnp.zeros((RWKV_LOW, 3 * RWKV_W), F32)
        wlow = wlow.at[0:RWKV_W_RANK, 0:RWKV_W].set(rwkv_w2[l])
        wlow = wlow.at[RWKV_W_RANK:RWKV_W_RANK + RWKV_A_RANK, RWKV_W:2 * RWKV_W].set(rwkv_a2[l])
        wlow = wlow.at[RWKV_W_RANK + RWKV_A_RANK:, 2 * RWKV_W:].set(rwkv_g2[l])
        prep_consts = (rwkv_mu[l][None], rwkv_w0[l][None], rwkv_a0[l][None], rwkv_k_k[l][None], rwkv_k_a[l][None],
                       wlow.astype(BF16), bd)
        post_consts = (rwkv_r_k[l].reshape(1, RWKV_W), rwkv_ln_w[l][None], rwkv_ln_b[l][None], bd)
        o_parts, st_parts = [], []
        for prev, n_b, n_t, s0 in ((None, nb, t, jnp.zeros((nb, RWKV_HEADS, RWKV_N, RWKV_N), F32)),
                                   (state_rwkv_shift[l], ns, 1, state_rwkv[l])):
            r_, w_, k_, v_, kk_, kka_, g_ = _rwkv_prep(pr, prev, prep_consts, n_b, n_t, mp)
            yt, sf = _rwkv_scan(_to_chains(w_, kk_, n_b, n_t), _to_chains(kka_, k_, n_b, n_t),
                                _to_chains(r_, r_, n_b, n_t), _v_to_chains(v_, n_b, n_t), _state_to_chains(s0, n_b))
            o_parts.append(_rwkv_post(_y_from_chains(yt, n_b, n_t), r_, k_, v_, g_, *post_consts))
            st_parts.append(_state_from_chains(sf, n_b))
        orw = jnp.concatenate(o_parts, axis=0)
        new_p[1].append(st_parts[0])
        new_s[1].append(st_parts[1])
        new_p[2].append(pr[:mp].reshape(nb, t, RWKV_COLS)[:, -1])
        new_s[2].append(pr[mp:])

        lre = s5_a_re[l].reshape(1, S5_N)
        lim = s5_a_im[l].reshape(1, S5_N)
        ldt = jnp.broadcast_to(s5_log_dt[l][:, None], (S5_GROUPS, S5_STATE)).reshape(1, S5_N)
        bre = _block_diag_expand(jnp.swapaxes(s5_b_re[l], 1, 2))
        bim = _block_diag_expand(jnp.swapaxes(s5_b_im[l], 1, 2))
        cre = _block_diag_expand(jnp.swapaxes(s5_c_re[l], 1, 2)).astype(BF16)
        cim = _block_diag_expand(jnp.swapaxes(s5_c_im[l], 1, 2)).astype(BF16)
        s5_args = (lre, lim, ldt, bre, bim, cre, cim, s5_d[l][None], s5_glu_w[l].astype(BF16), s5_glu_b[l][None])
        u_p = ps[:mp].reshape(nb, t, S5_WIDTH).transpose(1, 0, 2).reshape(mp, S5_WIDTH)
        os_p, xf_p = _s5(u_p, jnp.zeros((nb, 2 * S5_N), F32), *s5_args, nb, t)
        os_p = os_p.reshape(t, nb, S5_WIDTH).transpose(1, 0, 2).reshape(mp, S5_WIDTH)
        x0_s = jnp.concatenate([state_s5_re[l].reshape(ns, S5_N), state_s5_im[l].reshape(ns, S5_N)], axis=1)
        os_s, xf_s = _s5(ps[mp:], x0_s, *s5_args, ns, 1)
        os5 = jnp.concatenate([os_p, os_s], axis=0)
        new_p[3].append(xf_p[:, :S5_N].reshape(nb, S5_GROUPS, S5_STATE))
        new_p[4].append(xf_p[:, S5_N:].reshape(nb, S5_GROUPS, S5_STATE))
        new_s[3].append(xf_s[:, :S5_N].reshape(ns, S5_GROUPS, S5_STATE))
        new_s[4].append(xf_s[:, S5_N:].reshape(ns, S5_GROUPS, S5_STATE))

        x = _merge(x, og, orw, os5, gates, w_br_gla[l].astype(BF16), w_br_rwkv[l].astype(BF16),
                   w_br_s5[l].astype(BF16), w_out[l].astype(BF16), tm_big)
        x = _ffn(x, norm_ffn[l][None], ffn_w1[l].astype(BF16), ffn_w3[l].astype(BF16), ffn_w2[l].astype(BF16),
                 tm_big)

    y = _final_norm(x, final_norm[None], tm_big)
    y_prompt = y[:mp].reshape(nb, t, d)[:, N_META:]
    y_sample = y[mp:].reshape(ns, 1, d)
    outs_p = [jnp.stack(a, axis=0) for a in new_p]
    outs_s = [jnp.stack(a, axis=0) for a in new_s]
    return (y_prompt, y_sample, *outs_p, *outs_s)
```

```python
import functools
import math

import jax
import jax.numpy as jnp
from jax import lax
from jax.experimental import pallas as pl
from jax.experimental.pallas import tpu as pltpu

F32 = jnp.float32
BF16 = jnp.bfloat16
HIGHEST = lax.Precision.HIGHEST

N_META = 16
NORM_EPS = 1e-6
GLA_HEADS = 4
GLA_DK = 64
GLA_DV = 128
GLA_QK = GLA_HEADS * GLA_DK
GLA_V = GLA_HEADS * GLA_DV
GLA_GATE_RANK = 16
GLA_TAU = 16.0
GLA_SUB = 16
RWKV_HEADS = 8
RWKV_N = 64
RWKV_W = RWKV_HEADS * RWKV_N
RWKV_W_RANK = 64
RWKV_A_RANK = 64
RWKV_G_RANK = 128
RWKV_LOW = RWKV_W_RANK + RWKV_A_RANK + RWKV_G_RANK
RWKV_COLS = 3 * RWKV_W + RWKV_LOW
RWKV_GN_EPS = 64e-5
S5_GROUP = 16
S5_GROUPS = 32
S5_WIDTH = S5_GROUPS * S5_GROUP
S5_STATE = 64
S5_N = S5_GROUPS * S5_STATE

LANES = 128
SUBLANES = 8
CHAINS = LANES // 2
HALF_N = RWKV_N // 2
VTILES = HALF_N // SUBLANES
S5_SLABS = S5_WIDTH // LANES
S5_SLAB_N = S5_N // S5_SLABS
GLA_ZPAD = LANES
GLA_COLS = 2 * GLA_QK + 2 * GLA_V + GLA_ZPAD
VMEM_LIMIT = 56 << 20


def _params(sem):
    return pltpu.CompilerParams(dimension_semantics=sem, vmem_limit_bytes=VMEM_LIMIT)


def _const_spec(shape):
    nd = len(shape)
    return pl.BlockSpec(shape, lambda *_: (0,) * nd, pipeline_mode=pl.Buffered(1))


def _row_spec(tm, cols):
    return pl.BlockSpec((tm, cols), lambda i: (i, 0))


def _pick_block(n, mult, cap):
    best = None
    for d in range(mult, min(n, cap) + 1, mult):
        if n % d == 0:
            best = d
    assert best is not None, (n, mult, cap)
    return best


def _sigmoid(x):
    return 1.0 / (1.0 + jnp.exp(-x))


def _softplus(x):
    return jnp.maximum(x, 0.0) + jnp.log1p(jnp.exp(-jnp.abs(x)))


def _rms(x, g):
    return x * lax.rsqrt(jnp.mean(x * x, axis=-1, keepdims=True) + NORM_EPS) * g


def _dot(a, b):
    return jnp.dot(a, b, preferred_element_type=F32)


def _dot_t0(a, b):
    return lax.dot_general(a, b, (((0,), (0,)), ((), ())), preferred_element_type=F32)


def _split3_rows(x):
    hi = x.astype(BF16).astype(F32)
    r1 = x - hi
    mid = r1.astype(BF16).astype(F32)
    lo = (r1 - mid).astype(BF16).astype(F32)
    return [hi, mid, lo]


def _inproj_body(x_ref, g_ref, wg_ref, wr_ref, ws_ref, wt_ref, og_ref, or_ref, os_ref, ot_ref):
    h = _rms(x_ref[...], g_ref[...]).astype(BF16)
    og_ref[...] = _dot(h, wg_ref[...])
    or_ref[...] = _dot(h, wr_ref[...])
    os_ref[...] = _dot(h, ws_ref[...])
    ot_ref[...] = _dot(h, wt_ref[...])


def _inproj(x, g, wg, wr, ws, wt):
    m, d = x.shape
    tm = _pick_block(m, SUBLANES, 256)
    widths = (wg.shape[1], wr.shape[1], ws.shape[1], wt.shape[1])
    return pl.pallas_call(
        _inproj_body,
        grid=(m // tm,),
        in_specs=[_row_spec(tm, d), _const_spec((1, d))] + [_const_spec(w.shape) for w in (wg, wr, ws, wt)],
        out_specs=[_row_spec(tm, c) for c in widths],
        out_shape=[jax.ShapeDtypeStruct((m, c), F32) for c in widths],
        compiler_params=_params(("parallel",)),
        name="inproj",
    )(x, g, wg, wr, ws, wt)


def _log_sigmoid(x):
    return jnp.minimum(x, 0.0) - jnp.log1p(jnp.exp(-jnp.abs(x)))


def _gla_finish(o, g, gn):
    parts = []
    for h in range(GLA_HEADS):
        oh = o[:, h * GLA_DV:(h + 1) * GLA_DV]
        parts.append(oh * lax.rsqrt(jnp.mean(oh * oh, axis=-1, keepdims=True) + NORM_EPS))
    on = jnp.concatenate(parts, axis=1)
    return on * gn * (g * _sigmoid(g))


def _gla_log_decay(z, w2_ref, gb_ref):
    zz = _dot(z.astype(BF16), w2_ref[...]) + gb_ref[...]
    return _log_sigmoid(zz) * (1.0 / GLA_TAU)


def _gla_chunk(qs, k, v, bc, s, segb, ones):
    c = GLA_SUB
    lane_head = lax.broadcasted_iota(jnp.int32, (c, GLA_QK), 1) // GLA_DK
    row_id = lax.broadcasted_iota(jnp.int32, (c, GLA_QK), 0)
    bl = bc[c - 1:c, :]
    qe = qs * jnp.exp(bc)
    lhs = jnp.concatenate([jnp.where(lane_head == h, qe, 0.0) for h in range(GLA_HEADS)], axis=0)
    oi = _dot(lhs.astype(BF16), s.astype(BF16))
    o_inter = jnp.concatenate([oi[h * c:(h + 1) * c] for h in range(GLA_HEADS)], axis=1)
    terms = []
    for jj in range(c):
        e = jnp.exp(jnp.minimum(bc - bc[jj:jj + 1, :], 0.0))
        terms.append(jnp.where(row_id >= jj, qs * k[jj:jj + 1, :] * e, 0.0))
    pw = _dot(jnp.concatenate(terms, axis=0).astype(BF16), segb)
    o_intra = pw[0:c] * v[0:1, :]
    for jj in range(1, c):
        o_intra = o_intra + pw[jj * c:(jj + 1) * c] * v[jj:jj + 1, :]
    kd = k * jnp.exp(bl - bc)
    vb = v.astype(BF16)
    kv = jnp.concatenate(
        [_dot_t0(kd[:, h * GLA_DK:(h + 1) * GLA_DK].astype(BF16), vb[:, h * GLA_DV:(h + 1) * GLA_DV])
         for h in range(GLA_HEADS)], axis=0)
    ef = jnp.concatenate(_split3_rows(jnp.exp(bl)) + [jnp.zeros((c - 3, GLA_QK), F32)], axis=0)
    dec = _dot_t0(ef.astype(BF16), ones)
    return o_inter + o_intra, dec * s + kv


def _gla_prompt_body(p_ref, w2_ref, gb_ref, gn_ref, tri_ref, segb_ref, ones_ref, o_ref, sf_ref, s_ref, *,
                     ngroup, unroll):
    c = GLA_SUB
    gr = c * unroll
    s_ref[...] = jnp.zeros_like(s_ref)

    def group(gi, carry):
        rows = pl.ds(pl.multiple_of(gi * gr, gr), gr)
        q = p_ref[rows, 0:GLA_QK]
        k = p_ref[rows, GLA_QK:2 * GLA_QK]
        v = p_ref[rows, 2 * GLA_QK:2 * GLA_QK + GLA_V]
        g = p_ref[rows, 2 * GLA_QK + GLA_V:2 * GLA_QK + 2 * GLA_V]
        z = p_ref[rows, 2 * GLA_QK + 2 * GLA_V:GLA_COLS]
        la = _gla_log_decay(z, w2_ref, gb_ref)
        bc = jnp.dot(tri_ref[...], la, precision=HIGHEST, preferred_element_type=F32)
        qs = q * (GLA_DK ** -0.5)
        s = s_ref[...]
        outs = []
        for u in range(unroll):
            sl = slice(u * c, (u + 1) * c)
            o_u, s = _gla_chunk(qs[sl], k[sl], v[sl], bc[sl], s, segb_ref[...], ones_ref[...])
            outs.append(o_u)
        o = outs[0] if unroll == 1 else jnp.concatenate(outs, axis=0)
        o_ref[rows, :] = _gla_finish(o, g, gn_ref[...]).astype(o_ref.dtype)
        s_ref[...] = s
        return carry

    lax.fori_loop(0, ngroup, group, 0)
    t = ngroup * gr
    if t < o_ref.shape[0]:
        o_ref[t:, :] = jnp.zeros((o_ref.shape[0] - t, GLA_V), o_ref.dtype)
    sf_ref[0] = s_ref[...]


def _gla_prompt(pg, w2p, gb, gn, consts, nb, t, tp, unroll):
    gr = GLA_SUB * unroll
    assert t % gr == 0
    tri, segb, ones = consts
    return pl.pallas_call(
        functools.partial(_gla_prompt_body, ngroup=t // gr, unroll=unroll),
        grid=(nb,),
        in_specs=[pl.BlockSpec((tp, GLA_COLS), lambda b: (b, 0)),
                  _const_spec(w2p.shape), _const_spec(gb.shape), _const_spec(gn.shape),
                  _const_spec(tri.shape), _const_spec(segb.shape), _const_spec(ones.shape)],
        out_specs=[pl.BlockSpec((tp, GLA_V), lambda b: (b, 0)),
                   pl.BlockSpec((1, GLA_QK, GLA_DV), lambda b: (b, 0, 0))],
        out_shape=[jax.ShapeDtypeStruct((nb * tp, GLA_V), BF16),
                   jax.ShapeDtypeStruct((nb, GLA_QK, GLA_DV), F32)],
        scratch_shapes=[pltpu.VMEM((GLA_QK, GLA_DV), F32)],
        compiler_params=_params(("parallel",)),
        name="gla_prompt",
    )(pg, w2p, gb, gn, tri, segb, ones)


def _gla_sample_body(p_ref, s0_ref, w2_ref, gb_ref, gn_ref, sel_ref, o_ref, s_ref, *, nrow):
    q = p_ref[:, 0:GLA_QK]
    k = p_ref[:, GLA_QK:2 * GLA_QK]
    v = p_ref[:, 2 * GLA_QK:2 * GLA_QK + GLA_V]
    g = p_ref[:, 2 * GLA_QK + GLA_V:2 * GLA_QK + 2 * GLA_V]
    z = p_ref[:, 2 * GLA_QK + 2 * GLA_V:GLA_COLS]
    a = jnp.exp(_gla_log_decay(z, w2_ref, gb_ref))
    qs = q * (GLA_DK ** -0.5)
    out_rows = []
    for i in range(nrow):
        pieces = _split3_rows(a[i:i + 1]) + _split3_rows(k[i:i + 1]) + _split3_rows(qs[i:i + 1])
        ef = jnp.concatenate(pieces + [jnp.zeros((16 - 9, GLA_QK), F32)], axis=0)
        cb = _dot_t0(ef.astype(BF16), sel_ref[...])
        ab = cb[:, 0:GLA_DV]
        kb = cb[:, GLA_DV:2 * GLA_DV]
        qb = cb[:, 2 * GLA_DV:3 * GLA_DV]
        vb = jnp.concatenate(
            [jnp.broadcast_to(v[i:i + 1, h * GLA_DV:(h + 1) * GLA_DV], (GLA_DK, GLA_DV)) for h in range(GLA_HEADS)],
            axis=0)
        s = ab * s0_ref[i] + kb * vb
        s_ref[i] = s
        op = qb * s
        out_rows.append(jnp.concatenate(
            [jnp.sum(op[h * GLA_DK:(h + 1) * GLA_DK], axis=0, keepdims=True) for h in range(GLA_HEADS)], axis=1))
    o = jnp.concatenate(out_rows, axis=0)
    o_ref[...] = _gla_finish(o, g, gn_ref[...]).astype(o_ref.dtype)


def _gla_sample(pg, s0, w2p, gb, gn, sel):
    ns = pg.shape[0]
    nrow = 16
    assert ns % nrow == 0
    return pl.pallas_call(
        functools.partial(_gla_sample_body, nrow=nrow),
        grid=(ns // nrow,),
        in_specs=[_row_spec(nrow, GLA_COLS),
                  pl.BlockSpec((nrow, GLA_QK, GLA_DV), lambda i: (i, 0, 0)),
                  _const_spec(w2p.shape), _const_spec(gb.shape), _const_spec(gn.shape), _const_spec(sel.shape)],
        out_specs=[_row_spec(nrow, GLA_V), pl.BlockSpec((nrow, GLA_QK, GLA_DV), lambda i: (i, 0, 0))],
        out_shape=[jax.ShapeDtypeStruct((ns, GLA_V), BF16), jax.ShapeDtypeStruct((ns, GLA_QK, GLA_DV), F32)],
        compiler_params=_params(("parallel",)),
        name="gla_sample",
    )(pg, s0, w2p, gb, gn, sel)


def _segsum(x, bd):
    hi = x.astype(BF16)
    lo = (x - hi.astype(F32)).astype(BF16)
    return _dot(hi, bd) + _dot(lo, bd)


def _rwkv_prep_math(p, prev, mu_ref, w0_ref, a0_ref, kkw_ref, kaw_ref, wlow_ref, bd_ref,
                    r_o, w_o, k_o, v_o, kk_o, kka_o, g_o):
    xs = p + (prev - p) * mu_ref[...]
    r = xs[:, 0:RWKV_W]
    k = xs[:, RWKV_W:2 * RWKV_W]
    v = xs[:, 2 * RWKV_W:3 * RWKV_W]
    zl = xs[:, 3 * RWKV_W:RWKV_COLS]
    lane = lax.broadcasted_iota(jnp.int32, zl.shape, 1)
    zt = jnp.where(lane < RWKV_W_RANK, jnp.tanh(zl),
                   jnp.where(lane < RWKV_W_RANK + RWKV_A_RANK, zl, _sigmoid(zl)))
    low = _dot(zt.astype(BF16), wlow_ref[...])
    w_log = -_softplus(-(w0_ref[...] + low[:, 0:RWKV_W])) - 0.5
    a = _sigmoid(a0_ref[...] + low[:, RWKV_W:2 * RWKV_W])
    kk = k * kkw_ref[...]
    kk = kk * lax.rsqrt(_segsum(kk * kk, bd_ref[...]) + 1e-12)
    r_o[...] = r
    w_o[...] = jnp.exp(-jnp.exp(w_log))
    k_o[...] = k * (1.0 + (a - 1.0) * kaw_ref[...])
    v_o[...] = v
    kk_o[...] = kk
    kka_o[...] = kk * a
    g_o[...] = low[:, 2 * RWKV_W:3 * RWKV_W]


def _rwkv_prep_seq_body(p_ref, *rest):
    carry_ref = rest[-1]

    @pl.when(pl.program_id(1) == 0)
    def _():
        carry_ref[...] = jnp.zeros_like(carry_ref)

    p = p_ref[...]
    row = lax.broadcasted_iota(jnp.int32, p.shape, 0)
    prev = jnp.where(row == 0, carry_ref[...], pltpu.roll(p, 1, axis=0))
    carry_ref[...] = p[p.shape[0] - 1:, :]
    _rwkv_prep_math(p, prev, *rest[:-1])


def _rwkv_prep_step_body(p_ref, pp_ref, *rest):
    _rwkv_prep_math(p_ref[...], pp_ref[...], *rest)


def _rwkv_prep(p, prev, consts, nb, t):
    vec = _const_spec((1, RWKV_W))
    mu, w0, a0, kkw, kaw, wlow, bd = consts
    const_specs = [_const_spec((1, RWKV_COLS)), vec, vec, vec, vec, _const_spec(wlow.shape), _const_spec(bd.shape)]
    if prev is None:
        tb = _pick_block(t, SUBLANES, 768)
        nt = t // tb
        rows = pl.BlockSpec((tb, RWKV_COLS), lambda b, j: (b * nt + j, 0))
        outs = pl.BlockSpec((tb, RWKV_W), lambda b, j: (b * nt + j, 0))
        return pl.pallas_call(
            _rwkv_prep_seq_body,
            grid=(nb, nt),
            in_specs=[rows] + const_specs,
            out_specs=[outs] * 7,
            out_shape=[jax.ShapeDtypeStruct((nb * t, RWKV_W), F32)] * 7,
            scratch_shapes=[pltpu.VMEM((1, RWKV_COLS), F32)],
            compiler_params=_params(("parallel", "arbitrary")),
            name="rwkv_prep_seq",
        )(p, *consts)
    tm = _pick_block(nb, SUBLANES, 256)
    return pl.pallas_call(
        _rwkv_prep_step_body,
        grid=(nb // tm,),
        in_specs=[_row_spec(tm, RWKV_COLS), _row_spec(tm, RWKV_COLS)] + const_specs,
        out_specs=[_row_spec(tm, RWKV_W)] * 7,
        out_shape=[jax.ShapeDtypeStruct((nb, RWKV_W), F32)] * 7,
        compiler_params=_params(("parallel",)),
        name="rwkv_prep_step",
    )(p, prev, *consts)


def _to_chain_body(x_ref, o_ref, z_ref, *, nb):
    tc = x_ref.shape[1]
    for b in range(nb):
        z_ref[b * RWKV_W:(b + 1) * RWKV_W, :] = x_ref[b].T
    for r in range(HALF_N):
        both = jnp.concatenate([z_ref[pl.ds(r, CHAINS, stride=RWKV_N), :],
                                z_ref[pl.ds(HALF_N + r, CHAINS, stride=RWKV_N), :]], axis=0)
        o_ref[pl.ds(r, tc, stride=HALF_N), :] = both.T


def _to_chain(x, nb, tp):
    assert nb * RWKV_HEADS == CHAINS and tp % LANES == 0
    tc = LANES
    return pl.pallas_call(
        functools.partial(_to_chain_body, nb=nb),
        grid=(tp // tc,),
        in_specs=[pl.BlockSpec((nb, tc, RWKV_W), lambda j: (0, j, 0))],
        out_specs=pl.BlockSpec((tc * HALF_N, LANES), lambda j: (j, 0)),
        out_shape=jax.ShapeDtypeStruct((tp * HALF_N, LANES), F32),
        scratch_shapes=[pltpu.VMEM((nb * RWKV_W, tc), F32)],
        compiler_params=_params(("parallel",)),
        name="rwkv_to_chain",
    )(x.reshape(nb, tp, RWKV_W))


def _from_chain_body(y_ref, o_ref, z_ref, *, nb):
    tc = o_ref.shape[1]
    for r in range(HALF_N):
        both = y_ref[pl.ds(r, tc, stride=HALF_N), :].T
        z_ref[pl.ds(r, CHAINS, stride=RWKV_N), :] = both[0:CHAINS]
        z_ref[pl.ds(HALF_N + r, CHAINS, stride=RWKV_N), :] = both[CHAINS:LANES]
    for b in range(nb):
        o_ref[b] = z_ref[b * RWKV_W:(b + 1) * RWKV_W, :].T


def _from_chain(y, nb, tp):
    tc = LANES
    out = pl.pallas_call(
        functools.partial(_from_chain_body, nb=nb),
        grid=(tp // tc,),
        in_specs=[pl.BlockSpec((tc * HALF_N, LANES), lambda j: (j, 0))],
        out_specs=pl.BlockSpec((nb, tc, RWKV_W), lambda j: (0, j, 0)),
        out_shape=jax.ShapeDtypeStruct((nb, tp, RWKV_W), F32),
        scratch_shapes=[pltpu.VMEM((nb * RWKV_W, tc), F32)],
        compiler_params=_params(("parallel",)),
        name="rwkv_from_chain",
    )(y)
    return out.reshape(nb * tp, RWKV_W)


def _rwkv_scan_body(w_ref, kk_ref, kka_ref, k_ref, r_ref, v_ref, s0_ref, yin_ref, y_ref, sf_ref, s_ref, u_ref, *, tc):
    del yin_ref
    j = pl.program_id(1)
    key_refs = (w_ref, kk_ref, kka_ref, k_ref, r_ref)
    low_half = lax.broadcasted_iota(jnp.int32, (HALF_N, LANES), 1) < CHAINS

    def spread(t, slot):
        for i, ref in enumerate(key_refs):
            x = ref[0, t]
            sw = pltpu.roll(x, CHAINS, axis=1)
            u_ref[slot, i, 0:HALF_N, :] = jnp.where(low_half, x, sw)
            u_ref[slot, i, HALF_N:RWKV_N, :] = jnp.where(low_half, sw, x)

    @pl.when(j == 0)
    def _():
        s_ref[...] = s0_ref[0]

    spread(0, 0)

    nsum = 4

    def total(parts):
        return (parts[0] + parts[1]) + (parts[2] + parts[3])

    def token(t, carry):
        slot = t % 2
        acc = [[None] * nsum for _ in range(VTILES)]
        for key in range(RWKV_N):
            kk = u_ref[slot, 1, key:key + 1, :]
            for i in range(VTILES):
                term = s_ref[i, key] * kk
                acc[i][key % nsum] = term if key < nsum else acc[i][key % nsum] + term
        sk = [total(a) for a in acc]
        vv = [v_ref[0, t, i * SUBLANES:(i + 1) * SUBLANES, :] for i in range(VTILES)]
        yacc = [[None] * nsum for _ in range(VTILES)]
        for key in range(RWKV_N):
            w = u_ref[slot, 0, key:key + 1, :]
            kka = u_ref[slot, 2, key:key + 1, :]
            k = u_ref[slot, 3, key:key + 1, :]
            r = u_ref[slot, 4, key:key + 1, :]
            for i in range(VTILES):
                sn = s_ref[i, key] * w - sk[i] * kka + vv[i] * k
                s_ref[i, key] = sn
                term = sn * r
                yacc[i][key % nsum] = term if key < nsum else yacc[i][key % nsum] + term
        for i in range(VTILES):
            y_ref[0, t, i * SUBLANES:(i + 1) * SUBLANES, :] = total(yacc[i])
        spread(jnp.minimum(t + 1, tc - 1), 1 - slot)
        return carry

    lax.fori_loop(0, tc, token, 0)

    @pl.when(j == pl.num_programs(1) - 1)
    def _():
        sf_ref[0] = s_ref[...]


def _rwkv_scan(w, kk, kka, k, r, v, s0, t):
    g, tp = w.shape[0], w.shape[1]
    tc = _pick_block(t, 1, 48)
    tok = pl.BlockSpec((1, tc, HALF_N, LANES), lambda i, j: (i, j, 0, 0))
    st = pl.BlockSpec((1, VTILES, RWKV_N, SUBLANES, LANES), lambda i, j: (i, 0, 0, 0, 0))
    y0 = jnp.zeros((g, tp, HALF_N, LANES), F32)
    return pl.pallas_call(
        functools.partial(_rwkv_scan_body, tc=tc),
        grid=(g, t // tc),
        in_specs=[tok] * 6 + [st, pl.BlockSpec(memory_space=pl.ANY)],
        out_specs=[tok, st],
        out_shape=[jax.ShapeDtypeStruct((g, tp, HALF_N, LANES), F32),
                   jax.ShapeDtypeStruct((g, VTILES, RWKV_N, SUBLANES, LANES), F32)],
        scratch_shapes=[pltpu.VMEM((VTILES, RWKV_N, SUBLANES, LANES), F32),
                        pltpu.VMEM((2, 5, RWKV_N, LANES), F32)],
        input_output_aliases={7: 0},
        compiler_params=_params(("parallel", "arbitrary")),
        name="rwkv_scan",
    )(w, kk, kka, k, r, v, s0, y0)


def _rwkv_post_body(y_ref, r_ref, k_ref, v_ref, g_ref, rk_ref, lnw_ref, lnb_ref, bd_ref, o_ref):
    y = y_ref[...]
    bd = bd_ref[...]
    inv_n = 1.0 / RWKV_N
    yc = y - _segsum(y, bd) * inv_n
    var = _segsum(yc * yc, bd) * inv_n
    yn = yc * lax.rsqrt(var + RWKV_GN_EPS) * lnw_ref[...] + lnb_ref[...]
    bonus = _segsum(r_ref[...] * k_ref[...] * rk_ref[...], bd) * v_ref[...]
    o_ref[...] = ((yn + bonus) * g_ref[...]).astype(o_ref.dtype)


def _rwkv_post(y, r, k, v, g, rk, lnw, lnb, bd):
    m = y.shape[0]
    tm = _pick_block(m, 16, 512)
    vec = _const_spec((1, RWKV_W))
    return pl.pallas_call(
        _rwkv_post_body,
        grid=(m // tm,),
        in_specs=[_row_spec(tm, RWKV_W)] * 5 + [vec, vec, vec, _const_spec(bd.shape)],
        out_specs=_row_spec(tm, RWKV_W),
        out_shape=jax.ShapeDtypeStruct((m, RWKV_W), BF16),
        compiler_params=_params(("parallel",)),
        name="rwkv_post",
    )(y, r, k, v, g, rk, lnw, lnb, bd)


def _sample_to_chain(x, nb):
    bg = CHAINS // RWKV_HEADS
    g = nb // bg
    x = x.reshape(g, bg, RWKV_HEADS, 2, HALF_N).transpose(0, 4, 3, 1, 2)
    return x.reshape(g, 1, HALF_N, LANES)


def _sample_from_chain(y, nb):
    bg = CHAINS // RWKV_HEADS
    g = nb // bg
    y = y.reshape(g, HALF_N, 2, bg, RWKV_HEADS).transpose(0, 3, 4, 2, 1)
    return y.reshape(nb, RWKV_W)


def _state_to_chains(s, nb):
    bg = CHAINS // RWKV_HEADS
    g = nb // bg
    s = s.reshape(g, bg, RWKV_HEADS, 2, VTILES, SUBLANES, RWKV_N).transpose(0, 4, 6, 5, 3, 1, 2)
    return s.reshape(g, VTILES, RWKV_N, SUBLANES, LANES)


def _state_from_chains(s, nb):
    bg = CHAINS // RWKV_HEADS
    g = nb // bg
    s = s.reshape(g, VTILES, RWKV_N, SUBLANES, 2, bg, RWKV_HEADS).transpose(0, 5, 6, 4, 1, 3, 2)
    return s.reshape(nb, RWKV_HEADS, RWKV_N, RWKV_N)


def _s5_body(u_ref, x0_ref, lre_ref, lim_ref, ldt_ref, bre_ref, bim_ref, cre_ref, cim_ref, d_ref, gw_ref, gbias_ref,
             oin_ref, o_ref, xf_ref, st_ref, ab_ref, bbar_ref, bu_ref, xs_ref, *, nrow, tc):
    del oin_ref
    j = pl.program_id(0)

    @pl.when(j == 0)
    def _():
        dt = jnp.exp(ldt_ref[...])
        lre = lre_ref[...]
        lim = lim_ref[...]
        mag = jnp.exp(lre * dt)
        are = mag * jnp.cos(lim * dt)
        aim = mag * jnp.sin(lim * dt)
        den = lre * lre + lim * lim
        nr = are - 1.0
        cre = (nr * lre + aim * lim) / den
        cim = (aim * lre - nr * lim) / den
        ab_ref[0:1, :] = are
        ab_ref[1:2, :] = aim
        for s in range(S5_SLABS):
            sl = slice(s * S5_SLAB_N, (s + 1) * S5_SLAB_N)
            bre = bre_ref[s]
            bim = bim_ref[s]
            bbar_ref[s, :, 0:S5_SLAB_N] = (bre * cre[:, sl] - bim * cim[:, sl]).astype(BF16)
            bbar_ref[s, :, S5_SLAB_N:2 * S5_SLAB_N] = (bre * cim[:, sl] + bim * cre[:, sl]).astype(BF16)
        st_ref[...] = x0_ref[...]

    for s in range(S5_SLABS):
        bu = _dot(u_ref[:, s * LANES:(s + 1) * LANES].astype(BF16), bbar_ref[s])
        bu_ref[:, s * S5_SLAB_N:(s + 1) * S5_SLAB_N] = bu[:, 0:S5_SLAB_N]
        bu_ref[:, S5_N + s * S5_SLAB_N:S5_N + (s + 1) * S5_SLAB_N] = bu[:, S5_SLAB_N:2 * S5_SLAB_N]
    are = ab_ref[0:1, :]
    aim = ab_ref[1:2, :]

    def token(t, carry):
        rows = pl.ds(pl.multiple_of(t * nrow, nrow), nrow)
        xr = st_ref[:, 0:S5_N]
        xi = st_ref[:, S5_N:2 * S5_N]
        nr = are * xr - aim * xi + bu_ref[rows, 0:S5_N]
        ni = are * xi + aim * xr + bu_ref[rows, S5_N:2 * S5_N]
        st_ref[:, 0:S5_N] = nr
        st_ref[:, S5_N:2 * S5_N] = ni
        xs_ref[rows, 0:S5_N] = nr
        xs_ref[rows, S5_N:2 * S5_N] = ni
        return carry

    lax.fori_loop(0, tc, token, 0)

    ys = []
    for s in range(S5_SLABS):
        sl = slice(s * S5_SLAB_N, (s + 1) * S5_SLAB_N)
        sli = slice(S5_N + s * S5_SLAB_N, S5_N + (s + 1) * S5_SLAB_N)
        ys.append(_dot(xs_ref[:, sl].astype(BF16), cre_ref[s]) - _dot(xs_ref[:, sli].astype(BF16), cim_ref[s]))
    y = jnp.concatenate(ys, axis=1) + d_ref[...] * u_ref[...]
    y = 0.5 * y * (1.0 + jnp.tanh(math.sqrt(2.0 / math.pi) * (y + 0.044715 * (y * y * y))))
    gate = _dot(y.astype(BF16), gw_ref[...]) + gbias_ref[...]
    o_ref[...] = (y * _sigmoid(gate)).astype(o_ref.dtype)

    @pl.when(j == pl.num_programs(0) - 1)
    def _():
        xf_ref[...] = st_ref[...]


def _s5(u, x0, lre, lim, ldt, bre, bim, cre, cim, d, gw, gbias, nrow, t):
    tc = _pick_block(t, 1, 48)
    rows = tc * nrow
    vec = _const_spec((1, S5_N))
    o0 = jnp.zeros((u.shape[0], S5_WIDTH), BF16)
    return pl.pallas_call(
        functools.partial(_s5_body, nrow=nrow, tc=tc),
        grid=(t // tc,),
        in_specs=[_row_spec(rows, S5_WIDTH), _const_spec((nrow, 2 * S5_N)), vec, vec, vec,
                  _const_spec(bre.shape), _const_spec(bim.shape), _const_spec(cre.shape), _const_spec(cim.shape),
                  _const_spec((1, S5_WIDTH)), _const_spec(gw.shape), _const_spec((1, S5_WIDTH)),
                  pl.BlockSpec(memory_space=pl.ANY)],
        out_specs=[_row_spec(rows, S5_WIDTH), pl.BlockSpec((nrow, 2 * S5_N), lambda i: (0, 0))],
        out_shape=[jax.ShapeDtypeStruct((u.shape[0], S5_WIDTH), BF16), jax.ShapeDtypeStruct((nrow, 2 * S5_N), F32)],
        scratch_shapes=[pltpu.VMEM((nrow, 2 * S5_N), F32), pltpu.VMEM((SUBLANES, S5_N), F32),
                        pltpu.VMEM((S5_SLABS, LANES, 2 * S5_SLAB_N), BF16), pltpu.VMEM((rows, 2 * S5_N), F32),
                        pltpu.VMEM((rows, 2 * S5_N), F32)],
        input_output_aliases={12: 0},
        compiler_params=_params(("arbitrary",)),
        name="s5",
    )(u, x0, lre, lim, ldt, bre, bim, cre, cim, d, gw, gbias, o0)


def _merge_body(x_ref, og_ref, or_ref, os_ref, gt_ref, wg_ref, wr_ref, ws_ref, wo_ref, o_ref, *, d):
    m = (_sigmoid(gt_ref[:, 0:d]) * _dot(og_ref[...], wg_ref[...])
         + _sigmoid(gt_ref[:, d:2 * d]) * _dot(or_ref[...], wr_ref[...])
         + _sigmoid(gt_ref[:, 2 * d:3 * d]) * _dot(os_ref[...], ws_ref[...]))
    o_ref[...] = x_ref[...] + _dot(m.astype(BF16), wo_ref[...])


def _merge(x, og, orw, os5, gates, wg, wr, ws, wo):
    m, d = x.shape
    tm = _pick_block(m, 16, 512)
    return pl.pallas_call(
        functools.partial(_merge_body, d=d),
        grid=(m // tm,),
        in_specs=[_row_spec(tm, d), _row_spec(tm, GLA_V), _row_spec(tm, RWKV_W), _row_spec(tm, S5_WIDTH),
                  _row_spec(tm, 3 * d)] + [_const_spec(w.shape) for w in (wg, wr, ws, wo)],
        out_specs=_row_spec(tm, d),
        out_shape=jax.ShapeDtypeStruct((m, d), F32),
        compiler_params=_params(("parallel",)),
        name="merge",
    )(x, og, orw, os5, gates, wg, wr, ws, wo)


def _ffn_body(x_ref, g_ref, w1_ref, w3_ref, w2_ref, o_ref, *, dff, cf):
    x = x_ref[...]
    h = _rms(x, g_ref[...]).astype(BF16)
    acc = x
    for c0 in range(0, dff, cf):
        a = _dot(h, w1_ref[:, c0:c0 + cf])
        b = _dot(h, w3_ref[:, c0:c0 + cf])
        acc = acc + _dot((a * _sigmoid(a) * b).astype(BF16), w2_ref[c0:c0 + cf, :])
    o_ref[...] = acc


def _ffn(x, g, w1, w3, w2):
    m, d = x.shape
    tm = _pick_block(m, SUBLANES, 1024)
    dff = w1.shape[1]
    cf = _pick_block(dff, LANES, 512)
    return pl.pallas_call(
        functools.partial(_ffn_body, dff=dff, cf=cf),
        grid=(m // tm,),
        in_specs=[_row_spec(tm, d), _const_spec((1, d)), _const_spec(w1.shape), _const_spec(w3.shape),
                  _const_spec(w2.shape)],
        out_specs=_row_spec(tm, d),
        out_shape=jax.ShapeDtypeStruct((m, d), F32),
        compiler_params=_params(("parallel",)),
        name="ffn",
    )(x, g, w1, w3, w2)


def _final_norm_body(x_ref, g_ref, o_ref):
    o_ref[...] = _rms(x_ref[...], g_ref[...])


def _final_norm(x, g):
    m, d = x.shape
    tm = _pick_block(m, SUBLANES, 1024)
    return pl.pallas_call(
        _final_norm_body,
        grid=(m // tm,),
        in_specs=[_row_spec(tm, d), _const_spec((1, d))],
        out_specs=_row_spec(tm, d),
        out_shape=jax.ShapeDtypeStruct((m, d), F32),
        compiler_params=_params(("parallel",)),
        name="final_norm",
    )(x, g)


def _block_ones(n, blk, dtype):
    i = jnp.arange(n) // blk
    return (i[:, None] == i[None, :]).astype(dtype)


def _gla_consts(unroll):
    c = GLA_SUB
    i = jnp.arange(c * unroll)
    tri = ((i[:, None] >= i[None, :]) & (i[:, None] // c == i[None, :] // c)).astype(F32)
    segb = (jnp.arange(GLA_QK)[:, None] // GLA_DK == jnp.arange(GLA_V)[None, :] // GLA_DV).astype(BF16)
    ones = jnp.ones((c, GLA_DV), BF16)
    sel = (jnp.arange(16)[:, None] // 3 == jnp.arange(3 * GLA_DV)[None, :] // GLA_DV).astype(BF16)
    return tri, segb, ones, sel


def _block_diag_slabs(p, nslab):
    g, a, b = p.shape
    gs = g // nslab
    eye = jnp.eye(gs, dtype=p.dtype)
    p = p.reshape(nslab, gs, a, b)
    return (p[:, :, :, None, :] * eye[None, :, None, :, None]).reshape(nslab, gs * a, gs * b)


def kernel(x_prompt, x_sample, state_gla, state_rwkv, state_rwkv_shift, state_s5_re, state_s5_im, meta_tokens, norm_mix, norm_ffn, w_in, gla_gate_w2, gla_gate_b, gla_norm, rwkv_mu, rwkv_w0, rwkv_w2, rwkv_a0, rwkv_a2, rwkv_g2, rwkv_k_k, rwkv_k_a, rwkv_r_k, rwkv_ln_w, rwkv_ln_b, s5_a_re, s5_a_im, s5_log_dt, s5_b_re, s5_b_im, s5_c_re, s5_c_im, s5_d, s5_glu_w, s5_glu_b, w_br_gla, w_br_rwkv, w_br_s5, w_out, ffn_w1, ffn_w3, ffn_w2, final_norm):
    nb, seq, d = x_prompt.shape
    ns = x_sample.shape[0]
    assert x_sample.shape[1] == 1
    depth = w_in.shape[0]
    t = seq + N_META
    tp = -(-t // LANES) * LANES
    mp = nb * tp
    assert nb * RWKV_HEADS == CHAINS and (ns * RWKV_HEADS) % CHAINS == 0

    meta = jnp.broadcast_to(meta_tokens.astype(x_prompt.dtype)[None], (nb, N_META, d))
    xp = jnp.concatenate([meta, x_prompt, jnp.zeros((nb, tp - t, d), x_prompt.dtype)], axis=1).reshape(mp, d)
    xs = x_sample.reshape(ns, d)

    gla_unroll = 3 if t % (3 * GLA_SUB) == 0 else 1
    tri, segb, ones, sel = _gla_consts(gla_unroll)
    bd = _block_ones(RWKV_W, RWKV_N, BF16)
    o_q, o_z = 0, 2 * GLA_QK + 2 * GLA_V
    o_r = o_z + GLA_GATE_RANK
    o_s = o_r + RWKV_COLS
    o_t = o_s + S5_WIDTH

    new_p = [[] for _ in range(5)]
    new_s = [[] for _ in range(5)]
    for l in range(depth):
        wl = w_in[l]
        wg = jnp.concatenate([wl[:, o_q:o_r], jnp.zeros((d, GLA_ZPAD - GLA_GATE_RANK), F32)], axis=1).astype(BF16)
        w_groups = (wg, wl[:, o_r:o_s].astype(BF16), wl[:, o_s:o_t].astype(BF16), wl[:, o_t:].astype(BF16))
        pg_p, pr_p, ps_p, gt_p = _inproj(xp, norm_mix[l][None], *w_groups)
        pg_s, pr_s, ps_s, gt_s = _inproj(xs, norm_mix[l][None], *w_groups)

        w2p = jnp.concatenate([gla_gate_w2[l], jnp.zeros((GLA_ZPAD - GLA_GATE_RANK, GLA_QK), F32)], axis=0).astype(BF16)
        gb = gla_gate_b[l][None]
        gn = gla_norm[l][None]
        og_p, gla_p = _gla_prompt(pg_p, w2p, gb, gn, (tri, segb, ones), nb, t, tp, gla_unroll)
        og_s, gla_s = _gla_sample(pg_s, state_gla[l].reshape(ns, GLA_QK, GLA_DV), w2p, gb, gn, sel)
        new_p[0].append(gla_p.reshape(nb, GLA_HEADS, GLA_DK, GLA_DV))
        new_s[0].append(gla_s.reshape(ns, GLA_HEADS, GLA_DK, GLA_DV))

        wlow = jnp.zeros((RWKV_LOW, 3 * RWKV_W), F32)
        wlow = wlow.at[0:RWKV_W_RANK, 0:RWKV_W].set(rwkv_w2[l])
        wlow = wlow.at[RWKV_W_RANK:RWKV_W_RANK + RWKV_A_RANK, RWKV_W:2 * RWKV_W].set(rwkv_a2[l])
        wlow = wlow.at[RWKV_W_RANK + RWKV_A_RANK:, 2 * RWKV_W:].set(rwkv_g2[l])
        prep_consts = (rwkv_mu[l][None], rwkv_w0[l][None], rwkv_a0[l][None], rwkv_k_k[l][None], rwkv_k_a[l][None],
                       wlow.astype(BF16), bd)
        post_consts = (rwkv_r_k[l].reshape(1, RWKV_W), rwkv_ln_w[l][None], rwkv_ln_b[l][None], bd)
        r_, w_, k_, v_, kk_, kka_, g_ = _rwkv_prep(pr_p, None, prep_consts, nb, tp)
        chain = [_to_chain(a, nb, tp).reshape(1, tp, HALF_N, LANES) for a in (w_, kk_, kka_, k_, r_, v_)]
        yt, sf = _rwkv_scan(*chain, _state_to_chains(jnp.zeros((nb, RWKV_HEADS, RWKV_N, RWKV_N), F32), nb), t)
        orw_p = _rwkv_post(_from_chain(yt.reshape(tp * HALF_N, LANES), nb, tp), r_, k_, v_, g_, *post_consts)
        new_p[1].append(_state_from_chains(sf, nb))
        new_p[2].append(pr_p.reshape(nb, tp, RWKV_COLS)[:, t - 1])
        r_, w_, k_, v_, kk_, kka_, g_ = _rwkv_prep(pr_s, state_rwkv_shift[l], prep_consts, ns, 1)
        chain = [_sample_to_chain(a, ns) for a in (w_, kk_, kka_, k_, r_, v_)]
        yt, sf = _rwkv_scan(*chain, _state_to_chains(state_rwkv[l], ns), 1)
        orw_s = _rwkv_post(_sample_from_chain(yt, ns), r_, k_, v_, g_, *post_consts)
        new_s[1].append(_state_from_chains(sf, ns))
        new_s[2].append(pr_s)

        lre = s5_a_re[l].reshape(1, S5_N)
        lim = s5_a_im[l].reshape(1, S5_N)
        ldt = jnp.broadcast_to(s5_log_dt[l][:, None], (S5_GROUPS, S5_STATE)).reshape(1, S5_N)
        bre = _block_diag_slabs(jnp.swapaxes(s5_b_re[l], 1, 2), S5_SLABS)
        bim = _block_diag_slabs(jnp.swapaxes(s5_b_im[l], 1, 2), S5_SLABS)
        cre = _block_diag_slabs(jnp.swapaxes(s5_c_re[l], 1, 2), S5_SLABS).astype(BF16)
        cim = _block_diag_slabs(jnp.swapaxes(s5_c_im[l], 1, 2), S5_SLABS).astype(BF16)
        s5_args = (lre, lim, ldt, bre, bim, cre, cim, s5_d[l][None], s5_glu_w[l].astype(BF16), s5_glu_b[l][None])
        u_p = ps_p.reshape(nb, tp, S5_WIDTH).transpose(1, 0, 2).reshape(mp, S5_WIDTH)
        os_p, xf_p = _s5(u_p, jnp.zeros((nb, 2 * S5_N), F32), *s5_args, nb, t)
        os_p = os_p.reshape(tp, nb, S5_WIDTH).transpose(1, 0, 2).reshape(mp, S5_WIDTH)
        x0_s = jnp.concatenate([state_s5_re[l].reshape(ns, S5_N), state_s5_im[l].reshape(ns, S5_N)], axis=1)
        os_s, xf_s = _s5(ps_s, x0_s, *s5_args, ns, 1)
        new_p[3].append(xf_p[:, :S5_N].reshape(nb, S5_GROUPS, S5_STATE))
        new_p[4].append(xf_p[:, S5_N:].reshape(nb, S5_GROUPS, S5_STATE))
        new_s[3].append(xf_s[:, :S5_N].reshape(ns, S5_GROUPS, S5_STATE))
        new_s[4].append(xf_s[:, S5_N:].reshape(ns, S5_GROUPS, S5_STATE))

        w_merge = (w_br_gla[l].astype(BF16), w_br_rwkv[l].astype(BF16), w_br_s5[l].astype(BF16), w_out[l].astype(BF16))
        w_ffn = (norm_ffn[l][None], ffn_w1[l].astype(BF16), ffn_w3[l].astype(BF16), ffn_w2[l].astype(BF16))
        xp = _ffn(_merge(xp, og_p, orw_p, os_p, gt_p, *w_merge), *w_ffn)
        xs = _ffn(_merge(xs, og_s, orw_s, os_s, gt_s, *w_merge), *w_ffn)

    y_prompt = _final_norm(xp, final_norm[None]).reshape(nb, tp, d)[:, N_META:t]
    y_sample = _final_norm(xs, final_norm[None]).reshape(ns, 1, d)
    outs_p = [jnp.stack(a, axis=0) for a in new_p]
    outs_s = [jnp.stack(a, axis=0) for a in new_s]
    return (y_prompt, y_sample, *outs_p, *outs_s)
```

```python
import functools
import math

import jax
import jax.numpy as jnp
from jax import lax
from jax.experimental import pallas as pl
from jax.experimental.pallas import tpu as pltpu

F32 = jnp.float32
BF16 = jnp.bfloat16
HIGHEST = lax.Precision.HIGHEST

N_META = 16
NORM_EPS = 1e-6
GLA_HEADS = 4
GLA_DK = 64
GLA_DV = 128
GLA_QK = GLA_HEADS * GLA_DK
GLA_V = GLA_HEADS * GLA_DV
GLA_GATE_RANK = 16
GLA_TAU = 16.0
GLA_SUB = 16
RWKV_HEADS = 8
RWKV_N = 64
RWKV_W = RWKV_HEADS * RWKV_N
RWKV_W_RANK = 64
RWKV_A_RANK = 64
RWKV_G_RANK = 128
RWKV_LOW = RWKV_W_RANK + RWKV_A_RANK + RWKV_G_RANK
RWKV_COLS = 3 * RWKV_W + RWKV_LOW
RWKV_GN_EPS = 64e-5
S5_GROUP = 16
S5_GROUPS = 32
S5_WIDTH = S5_GROUPS * S5_GROUP
S5_STATE = 64
S5_N = S5_GROUPS * S5_STATE

LANES = 128
SUBLANES = 8
CHAINS = LANES // 2
HALF_N = RWKV_N // 2
VTILES = HALF_N // SUBLANES
S5_SLABS = S5_WIDTH // LANES
S5_SLAB_N = S5_N // S5_SLABS
GLA_ZPAD = LANES
GLA_COLS = 2 * GLA_QK + 2 * GLA_V + GLA_ZPAD
VMEM_LIMIT = 56 << 20


def _params(sem):
    return pltpu.CompilerParams(dimension_semantics=sem, vmem_limit_bytes=VMEM_LIMIT)


def _const_spec(shape):
    nd = len(shape)
    return pl.BlockSpec(shape, lambda *_: (0,) * nd, pipeline_mode=pl.Buffered(1))


def _row_spec(tm, cols):
    return pl.BlockSpec((tm, cols), lambda i: (i, 0))


def _pick_block(n, mult, cap):
    best = None
    for d in range(mult, min(n, cap) + 1, mult):
        if n % d == 0:
            best = d
    assert best is not None, (n, mult, cap)
    return best


def _sigmoid(x):
    return 1.0 / (1.0 + jnp.exp(-x))


def _softplus(x):
    return jnp.maximum(x, 0.0) + jnp.log1p(jnp.exp(-jnp.abs(x)))


def _rms(x, g):
    return x * lax.rsqrt(jnp.mean(x * x, axis=-1, keepdims=True) + NORM_EPS) * g


def _dot(a, b):
    return jnp.dot(a, b, preferred_element_type=F32)


def _dot_t0(a, b):
    return lax.dot_general(a, b, (((0,), (0,)), ((), ())), preferred_element_type=F32)


def _split3_rows(x):
    hi = x.astype(BF16).astype(F32)
    r1 = x - hi
    mid = r1.astype(BF16).astype(F32)
    lo = (r1 - mid).astype(BF16).astype(F32)
    return [hi, mid, lo]


def _inproj_body(x_ref, g_ref, wg_ref, wr_ref, ws_ref, wt_ref, og_ref, or_ref, os_ref, ot_ref):
    h = _rms(x_ref[...], g_ref[...]).astype(BF16)
    og_ref[...] = _dot(h, wg_ref[...])
    or_ref[...] = _dot(h, wr_ref[...])
    os_ref[...] = _dot(h, ws_ref[...])
    ot_ref[...] = _dot(h, wt_ref[...])


def _inproj(x, g, wg, wr, ws, wt):
    m, d = x.shape
    tm = _pick_block(m, SUBLANES, 256)
    widths = (wg.shape[1], wr.shape[1], ws.shape[1], wt.shape[1])
    return pl.pallas_call(
        _inproj_body,
        grid=(m // tm,),
        in_specs=[_row_spec(tm, d), _const_spec((1, d))] + [_const_spec(w.shape) for w in (wg, wr, ws, wt)],
        out_specs=[_row_spec(tm, c) for c in widths],
        out_shape=[jax.ShapeDtypeStruct((m, c), F32) for c in widths],
        compiler_params=_params(("parallel",)),
        name="inproj",
    )(x, g, wg, wr, ws, wt)


def _log_sigmoid(x):
    return jnp.minimum(x, 0.0) - jnp.log1p(jnp.exp(-jnp.abs(x)))


def _gla_finish(o, g, gn):
    parts = []
    for h in range(GLA_HEADS):
        oh = o[:, h * GLA_DV:(h + 1) * GLA_DV]
        parts.append(oh * lax.rsqrt(jnp.mean(oh * oh, axis=-1, keepdims=True) + NORM_EPS))
    on = jnp.concatenate(parts, axis=1)
    return on * gn * (g * _sigmoid(g))


def _gla_log_decay(z, w2_ref, gb_ref):
    zz = _dot(z.astype(BF16), w2_ref[...]) + gb_ref[...]
    return _log_sigmoid(zz) * (1.0 / GLA_TAU)


GLA_TILE = LANES


def _pad_heads(x):
    zero = jnp.zeros((x.shape[0], LANES - GLA_DK), x.dtype)
    parts = []
    for h in range(GLA_HEADS):
        parts += [x[:, h * GLA_DK:(h + 1) * GLA_DK], zero]
    return jnp.concatenate(parts, axis=1)


def _gla_intra_body(p_ref, w2_ref, gb_ref, gn_ref, tri_ref, tot_ref, first_ref, segb_ref,
                    oi_ref, qe_ref, kd_ref, vb_ref, gate_ref, ebl_ref):
    c = GLA_SUB
    q = p_ref[:, 0:GLA_QK]
    k = p_ref[:, GLA_QK:2 * GLA_QK]
    v = p_ref[:, 2 * GLA_QK:2 * GLA_QK + GLA_V]
    g = p_ref[:, 2 * GLA_QK + GLA_V:2 * GLA_QK + 2 * GLA_V]
    z = p_ref[:, 2 * GLA_QK + 2 * GLA_V:GLA_COLS]
    la = _gla_log_decay(z, w2_ref, gb_ref)
    bc = jnp.dot(tri_ref[...], la, precision=HIGHEST, preferred_element_type=F32)
    bl = jnp.dot(tot_ref[...], la, precision=HIGHEST, preferred_element_type=F32)
    qs = q * (GLA_DK ** -0.5)
    qe_ref[...] = _pad_heads(qs * jnp.exp(bc)).astype(BF16)
    kd_ref[...] = _pad_heads(k * jnp.exp(bl - bc)).astype(BF16)
    ebl_ref[...] = jnp.dot(first_ref[...], _pad_heads(jnp.exp(bl)), precision=HIGHEST, preferred_element_type=F32)
    vb_ref[...] = v.astype(BF16)
    gate_ref[...] = gn_ref[...] * (g * _sigmoid(g))
    row_id = lax.broadcasted_iota(jnp.int32, (c, GLA_QK), 0)
    nchunk = p_ref.shape[0] // c
    per_dot = 4
    for c0 in range(0, nchunk, per_dot):
        terms = []
        for u in range(c0, c0 + per_dot):
            sl = slice(u * c, (u + 1) * c)
            for jj in range(c):
                e = jnp.exp(jnp.minimum(bc[sl] - bc[u * c + jj:u * c + jj + 1, :], 0.0))
                terms.append(jnp.where(row_id >= jj, qs[sl] * k[u * c + jj:u * c + jj + 1, :] * e, 0.0))
        pw = _dot(jnp.concatenate(terms, axis=0).astype(BF16), segb_ref[...])
        for i, u in enumerate(range(c0, c0 + per_dot)):
            base = i * c * c
            acc = pw[base:base + c] * v[u * c:u * c + 1, :]
            for jj in range(1, c):
                acc = acc + pw[base + jj * c:base + (jj + 1) * c] * v[u * c + jj:u * c + jj + 1, :]
            oi_ref[u * c:(u + 1) * c, :] = acc


def _gla_intra(pg, w2p, gb, gn, consts):
    m = pg.shape[0]
    tm = GLA_TILE
    assert m % tm == 0
    tri, tot, first, segb = consts
    nck = tm // GLA_SUB
    wide = GLA_HEADS * LANES
    return pl.pallas_call(
        _gla_intra_body,
        grid=(m // tm,),
        in_specs=[_row_spec(tm, GLA_COLS), _const_spec(w2p.shape), _const_spec(gb.shape), _const_spec(gn.shape),
                  _const_spec(tri.shape), _const_spec(tot.shape), _const_spec(first.shape), _const_spec(segb.shape)],
        out_specs=[_row_spec(tm, GLA_V), _row_spec(tm, wide), _row_spec(tm, wide), _row_spec(tm, GLA_V),
                   _row_spec(tm, GLA_V), _row_spec(nck, wide)],
        out_shape=[jax.ShapeDtypeStruct((m, GLA_V), F32), jax.ShapeDtypeStruct((m, wide), BF16),
                   jax.ShapeDtypeStruct((m, wide), BF16), jax.ShapeDtypeStruct((m, GLA_V), BF16),
                   jax.ShapeDtypeStruct((m, GLA_V), F32), jax.ShapeDtypeStruct((m // GLA_SUB, wide), F32)],
        compiler_params=_params(("parallel",)),
        name="gla_intra",
    )(pg, w2p, gb, gn, tri, tot, first, segb)


def _gla_state_body(oi_ref, qe_ref, kd_ref, vb_ref, gate_ref, ebl_ref, o_ref, sf_ref, st_ref, *, ngroup, unroll):
    c = GLA_SUB
    gr = c * unroll

    st_ref[...] = jnp.zeros_like(st_ref)

    def group(gi, carry):
        rows = pl.ds(pl.multiple_of(gi * gr, gr), gr)
        qe = qe_ref[rows, :]
        kd = kd_ref[rows, :]
        vb = vb_ref[rows, :]
        st = [st_ref[h] for h in range(GLA_HEADS)]
        outs = []
        for u in range(unroll):
            sl = slice(u * c, (u + 1) * c)
            dec = ebl_ref[gi * unroll + u]
            parts = []
            for h in range(GLA_HEADS):
                hs = slice(h * LANES, (h + 1) * LANES)
                parts.append(lax.dot_general(qe[sl, hs], st[h].astype(BF16), (((1,), (1,)), ((), ())),
                                             preferred_element_type=F32))
                st[h] = st[h] * dec[:, hs] + _dot_t0(vb[sl, hs], kd[sl, hs])
            outs.append(jnp.concatenate(parts, axis=1))
        o = (outs[0] if unroll == 1 else jnp.concatenate(outs, axis=0)) + oi_ref[rows, :]
        normed = []
        for h in range(GLA_HEADS):
            oh = o[:, h * GLA_DV:(h + 1) * GLA_DV]
            normed.append(oh * lax.rsqrt(jnp.mean(oh * oh, axis=-1, keepdims=True) + NORM_EPS))
        o_ref[rows, :] = (jnp.concatenate(normed, axis=1) * gate_ref[rows, :]).astype(o_ref.dtype)
        for h in range(GLA_HEADS):
            st_ref[h] = st[h]
        return carry

    lax.fori_loop(0, ngroup, group, 0)
    t = ngroup * gr
    if t < o_ref.shape[0]:
        o_ref[t:, :] = jnp.zeros((o_ref.shape[0] - t, GLA_V), o_ref.dtype)
    sf_ref[0] = st_ref[...]


def _gla_state(oi, qe, kd, vb, gate, ebl, nb, t, tp, unroll):
    gr = GLA_SUB * unroll
    assert t % gr == 0
    wide = GLA_HEADS * LANES
    nck = tp // GLA_SUB
    seq = lambda cols: pl.BlockSpec((tp, cols), lambda b: (b, 0))
    return pl.pallas_call(
        functools.partial(_gla_state_body, ngroup=t // gr, unroll=unroll),
        grid=(nb,),
        in_specs=[seq(GLA_V), seq(wide), seq(wide), seq(GLA_V), seq(GLA_V),
                  pl.BlockSpec((nck, 1, wide), lambda b: (b, 0, 0))],
        out_specs=[seq(GLA_V), pl.BlockSpec((1, GLA_HEADS, GLA_DV, LANES), lambda b: (b, 0, 0, 0))],
        out_shape=[jax.ShapeDtypeStruct((nb * tp, GLA_V), BF16),
                   jax.ShapeDtypeStruct((nb, GLA_HEADS, GLA_DV, LANES), F32)],
        scratch_shapes=[pltpu.VMEM((GLA_HEADS, GLA_DV, LANES), F32)],
        compiler_params=_params(("parallel",)),
        name="gla_state",
    )(oi, qe, kd, vb, gate, ebl.reshape(nb * nck, 1, wide))


def _gla_sample_body(p_ref, s0_ref, w2_ref, gb_ref, gn_ref, sel_ref, o_ref, s_ref, *, nrow):
    q = p_ref[:, 0:GLA_QK]
    k = p_ref[:, GLA_QK:2 * GLA_QK]
    v = p_ref[:, 2 * GLA_QK:2 * GLA_QK + GLA_V]
    g = p_ref[:, 2 * GLA_QK + GLA_V:2 * GLA_QK + 2 * GLA_V]
    z = p_ref[:, 2 * GLA_QK + 2 * GLA_V:GLA_COLS]
    a = jnp.exp(_gla_log_decay(z, w2_ref, gb_ref))
    qs = q * (GLA_DK ** -0.5)
    out_rows = []
    for i in range(nrow):
        pieces = _split3_rows(a[i:i + 1]) + _split3_rows(k[i:i + 1]) + _split3_rows(qs[i:i + 1])
        ef = jnp.concatenate(pieces + [jnp.zeros((16 - 9, GLA_QK), F32)], axis=0)
        cb = _dot_t0(ef.astype(BF16), sel_ref[...])
        ab = cb[:, 0:GLA_DV]
        kb = cb[:, GLA_DV:2 * GLA_DV]
        qb = cb[:, 2 * GLA_DV:3 * GLA_DV]
        vb = jnp.concatenate(
            [jnp.broadcast_to(v[i:i + 1, h * GLA_DV:(h + 1) * GLA_DV], (GLA_DK, GLA_DV)) for h in range(GLA_HEADS)],
            axis=0)
        s = ab * s0_ref[i] + kb * vb
        s_ref[i] = s
        op = qb * s
        out_rows.append(jnp.concatenate(
            [jnp.sum(op[h * GLA_DK:(h + 1) * GLA_DK], axis=0, keepdims=True) for h in range(GLA_HEADS)], axis=1))
    o = jnp.concatenate(out_rows, axis=0)
    o_ref[...] = _gla_finish(o, g, gn_ref[...]).astype(o_ref.dtype)


def _gla_sample(pg, s0, w2p, gb, gn, sel):
    ns = pg.shape[0]
    nrow = 16
    assert ns % nrow == 0
    return pl.pallas_call(
        functools.partial(_gla_sample_body, nrow=nrow),
        grid=(ns // nrow,),
        in_specs=[_row_spec(nrow, GLA_COLS),
                  pl.BlockSpec((nrow, GLA_QK, GLA_DV), lambda i: (i, 0, 0)),
                  _const_spec(w2p.shape), _const_spec(gb.shape), _const_spec(gn.shape), _const_spec(sel.shape)],
        out_specs=[_row_spec(nrow, GLA_V), pl.BlockSpec((nrow, GLA_QK, GLA_DV), lambda i: (i, 0, 0))],
        out_shape=[jax.ShapeDtypeStruct((ns, GLA_V), BF16), jax.ShapeDtypeStruct((ns, GLA_QK, GLA_DV), F32)],
        compiler_params=_params(("parallel",)),
        name="gla_sample",
    )(pg, s0, w2p, gb, gn, sel)


def _segsum(x, bd):
    hi = x.astype(BF16)
    lo = (x - hi.astype(F32)).astype(BF16)
    return _dot(hi, bd) + _dot(lo, bd)


def _rwkv_prep_math(p, prev, mu_ref, w0_ref, a0_ref, kkw_ref, kaw_ref, wlow_ref, bd_ref,
                    r_o, w_o, k_o, v_o, kk_o, kka_o, g_o):
    xs = p + (prev - p) * mu_ref[...]
    r = xs[:, 0:RWKV_W]
    k = xs[:, RWKV_W:2 * RWKV_W]
    v = xs[:, 2 * RWKV_W:3 * RWKV_W]
    zl = xs[:, 3 * RWKV_W:RWKV_COLS]
    lane = lax.broadcasted_iota(jnp.int32, zl.shape, 1)
    zt = jnp.where(lane < RWKV_W_RANK, jnp.tanh(zl),
                   jnp.where(lane < RWKV_W_RANK + RWKV_A_RANK, zl, _sigmoid(zl)))
    low = _dot(zt.astype(BF16), wlow_ref[...])
    w_log = -_softplus(-(w0_ref[...] + low[:, 0:RWKV_W])) - 0.5
    a = _sigmoid(a0_ref[...] + low[:, RWKV_W:2 * RWKV_W])
    kk = k * kkw_ref[...]
    kk = kk * lax.rsqrt(_segsum(kk * kk, bd_ref[...]) + 1e-12)
    r_o[...] = r
    w_o[...] = jnp.exp(-jnp.exp(w_log))
    k_o[...] = k * (1.0 + (a - 1.0) * kaw_ref[...])
    v_o[...] = v
    kk_o[...] = kk
    kka_o[...] = kk * a
    g_o[...] = low[:, 2 * RWKV_W:3 * RWKV_W]


def _rwkv_prep_seq_body(p_ref, *rest):
    carry_ref = rest[-1]

    @pl.when(pl.program_id(1) == 0)
    def _():
        carry_ref[...] = jnp.zeros_like(carry_ref)

    p = p_ref[...]
    row = lax.broadcasted_iota(jnp.int32, p.shape, 0)
    prev = jnp.where(row == 0, carry_ref[...], pltpu.roll(p, 1, axis=0))
    carry_ref[...] = p[p.shape[0] - 1:, :]
    _rwkv_prep_math(p, prev, *rest[:-1])


def _rwkv_prep_step_body(p_ref, pp_ref, *rest):
    _rwkv_prep_math(p_ref[...], pp_ref[...], *rest)


def _rwkv_prep(p, prev, consts, nb, t):
    vec = _const_spec((1, RWKV_W))
    mu, w0, a0, kkw, kaw, wlow, bd = consts
    const_specs = [_const_spec((1, RWKV_COLS)), vec, vec, vec, vec, _const_spec(wlow.shape), _const_spec(bd.shape)]
    if prev is None:
        tb = _pick_block(t, SUBLANES, 768)
        nt = t // tb
        rows = pl.BlockSpec((tb, RWKV_COLS), lambda b, j: (b * nt + j, 0))
        outs = pl.BlockSpec((tb, RWKV_W), lambda b, j: (b * nt + j, 0))
        return pl.pallas_call(
            _rwkv_prep_seq_body,
            grid=(nb, nt),
            in_specs=[rows] + const_specs,
            out_specs=[outs] * 7,
            out_shape=[jax.ShapeDtypeStruct((nb * t, RWKV_W), F32)] * 7,
            scratch_shapes=[pltpu.VMEM((1, RWKV_COLS), F32)],
            compiler_params=_params(("parallel", "arbitrary")),
            name="rwkv_prep_seq",
        )(p, *consts)
    tm = _pick_block(nb, SUBLANES, 256)
    return pl.pallas_call(
        _rwkv_prep_step_body,
        grid=(nb // tm,),
        in_specs=[_row_spec(tm, RWKV_COLS), _row_spec(tm, RWKV_COLS)] + const_specs,
        out_specs=[_row_spec(tm, RWKV_W)] * 7,
        out_shape=[jax.ShapeDtypeStruct((nb, RWKV_W), F32)] * 7,
        compiler_params=_params(("parallel",)),
        name="rwkv_prep_step",
    )(p, prev, *consts)


def _to_chain_body(*refs, nb, narr):
    x_refs, o_refs, z_ref = refs[:narr], refs[narr:2 * narr], refs[2 * narr]
    tc = x_refs[0].shape[1]
    def stage1(i, b):
        z_ref[i, b * RWKV_W:(b + 1) * RWKV_W, :] = x_refs[i][b].T

    def stage2(i, a):
        tiles = []
        for r in range(a * SUBLANES, (a + 1) * SUBLANES):
            tiles += [z_ref[i, pl.ds(r, CHAINS, stride=RWKV_N), :],
                      z_ref[i, pl.ds(HALF_N + r, CHAINS, stride=RWKV_N), :]]
        o_refs[i][a] = jnp.concatenate(tiles, axis=0).T

    for i in range(narr):
        for b in range(nb):
            stage1(i, b)
    for i in range(narr):
        for a in range(VTILES):
            stage2(i, a)


def _to_chain(xs, nb, tp):
    assert nb * RWKV_HEADS == CHAINS and tp % LANES == 0
    tc = LANES
    narr = len(xs)
    outs = pl.pallas_call(
        functools.partial(_to_chain_body, nb=nb, narr=narr),
        grid=(tp // tc,),
        in_specs=[pl.BlockSpec((nb, tc, RWKV_W), lambda j: (0, j, 0))] * narr,
        out_specs=[pl.BlockSpec((VTILES, tc, SUBLANES * LANES), lambda j: (0, j, 0))] * narr,
        out_shape=[jax.ShapeDtypeStruct((VTILES, tp, SUBLANES * LANES), F32)] * narr,
        scratch_shapes=[pltpu.VMEM((narr, nb * RWKV_W, tc), F32)],
        compiler_params=_params(("parallel",)),
        name="rwkv_to_chain",
    )(*[x.reshape(nb, tp, RWKV_W) for x in xs])
    return [o.reshape(1, VTILES, tp, SUBLANES, LANES) for o in outs]


def _from_chain_body(y_ref, o_ref, z_ref, *, nb):
    tc = o_ref.shape[1]
    for r in range(HALF_N):
        a, lo = divmod(r, SUBLANES)
        both = y_ref[a, :, lo * LANES:(lo + 1) * LANES].T
        z_ref[pl.ds(r, CHAINS, stride=RWKV_N), :] = both[0:CHAINS]
        z_ref[pl.ds(HALF_N + r, CHAINS, stride=RWKV_N), :] = both[CHAINS:LANES]
    for b in range(nb):
        o_ref[b] = z_ref[b * RWKV_W:(b + 1) * RWKV_W, :].T


def _from_chain(y, nb, tp):
    tc = LANES
    out = pl.pallas_call(
        functools.partial(_from_chain_body, nb=nb),
        grid=(tp // tc,),
        in_specs=[pl.BlockSpec((VTILES, tc, SUBLANES * LANES), lambda j: (0, j, 0))],
        out_specs=pl.BlockSpec((nb, tc, RWKV_W), lambda j: (0, j, 0)),
        out_shape=jax.ShapeDtypeStruct((nb, tp, RWKV_W), F32),
        scratch_shapes=[pltpu.VMEM((nb * RWKV_W, tc), F32)],
        compiler_params=_params(("parallel",)),
        name="rwkv_from_chain",
    )(y)
    return out.reshape(nb * tp, RWKV_W)


def _rwkv_scan_body(w_ref, kk_ref, kka_ref, k_ref, r_ref, v_ref, s0_ref, yin_ref, y_ref, sf_ref, s_ref, u_ref, *, tc):
    del yin_ref
    j = pl.program_id(1)
    key_refs = (w_ref, kk_ref, kka_ref, k_ref, r_ref)
    low_half = lax.broadcasted_iota(jnp.int32, (SUBLANES, LANES), 1) < CHAINS

    def spread(t, slot):
        for i, ref in enumerate(key_refs):
            for a in range(VTILES):
                x = ref[0, a, t]
                sw = pltpu.roll(x, CHAINS, axis=1)
                r0 = a * SUBLANES
                u_ref[slot, i, r0:r0 + SUBLANES, :] = jnp.where(low_half, x, sw)
                u_ref[slot, i, HALF_N + r0:HALF_N + r0 + SUBLANES, :] = jnp.where(low_half, sw, x)

    @pl.when(j == 0)
    def _():
        s_ref[...] = s0_ref[0]

    spread(0, 0)

    nsum = 4

    def total(parts):
        return (parts[0] + parts[1]) + (parts[2] + parts[3])

    def token(t, carry):
        slot = t % 2
        acc = [[None] * nsum for _ in range(VTILES)]
        for key in range(RWKV_N):
            kk = u_ref[slot, 1, key:key + 1, :]
            for i in range(VTILES):
                term = s_ref[i, key] * kk
                acc[i][key % nsum] = term if key < nsum else acc[i][key % nsum] + term
        sk = [total(a) for a in acc]
        vv = [v_ref[0, i, t] for i in range(VTILES)]
        yacc = [[None] * nsum for _ in range(VTILES)]
        for key in range(RWKV_N):
            w = u_ref[slot, 0, key:key + 1, :]
            kka = u_ref[slot, 2, key:key + 1, :]
            k = u_ref[slot, 3, key:key + 1, :]
            r = u_ref[slot, 4, key:key + 1, :]
            for i in range(VTILES):
                sn = s_ref[i, key] * w - sk[i] * kka + vv[i] * k
                s_ref[i, key] = sn
                term = sn * r
                yacc[i][key % nsum] = term if key < nsum else yacc[i][key % nsum] + term
        for i in range(VTILES):
            y_ref[0, i, t] = total(yacc[i])
        spread(jnp.minimum(t + 1, tc - 1), 1 - slot)
        return carry

    lax.fori_loop(0, tc, token, 0)

    @pl.when(j == pl.num_programs(1) - 1)
    def _():
        sf_ref[0] = s_ref[...]


def _rwkv_scan(w, kk, kka, k, r, v, s0, t):
    g, tp = w.shape[0], w.shape[2]
    tc = _pick_block(t, 1, 48)
    tok = pl.BlockSpec((1, VTILES, tc, SUBLANES, LANES), lambda i, j: (i, 0, j, 0, 0))
    st = pl.BlockSpec((1, VTILES, RWKV_N, SUBLANES, LANES), lambda i, j: (i, 0, 0, 0, 0))
    y0 = jnp.zeros((g, VTILES, tp, SUBLANES, LANES), F32)
    return pl.pallas_call(
        functools.partial(_rwkv_scan_body, tc=tc),
        grid=(g, t // tc),
        in_specs=[tok] * 6 + [st, pl.BlockSpec(memory_space=pl.ANY)],
        out_specs=[tok, st],
        out_shape=[jax.ShapeDtypeStruct((g, VTILES, tp, SUBLANES, LANES), F32),
                   jax.ShapeDtypeStruct((g, VTILES, RWKV_N, SUBLANES, LANES), F32)],
        scratch_shapes=[pltpu.VMEM((VTILES, RWKV_N, SUBLANES, LANES), F32),
                        pltpu.VMEM((2, 5, RWKV_N, LANES), F32)],
        input_output_aliases={7: 0},
        compiler_params=_params(("parallel", "arbitrary")),
        name="rwkv_scan",
    )(w, kk, kka, k, r, v, s0, y0)


def _rwkv_post_body(y_ref, r_ref, k_ref, v_ref, g_ref, rk_ref, lnw_ref, lnb_ref, bd_ref, o_ref):
    y = y_ref[...]
    bd = bd_ref[...]
    inv_n = 1.0 / RWKV_N
    yc = y - _segsum(y, bd) * inv_n
    var = _segsum(yc * yc, bd) * inv_n
    yn = yc * lax.rsqrt(var + RWKV_GN_EPS) * lnw_ref[...] + lnb_ref[...]
    bonus = _segsum(r_ref[...] * k_ref[...] * rk_ref[...], bd) * v_ref[...]
    o_ref[...] = ((yn + bonus) * g_ref[...]).astype(o_ref.dtype)


def _rwkv_post(y, r, k, v, g, rk, lnw, lnb, bd):
    m = y.shape[0]
    tm = _pick_block(m, 16, 512)
    vec = _const_spec((1, RWKV_W))
    return pl.pallas_call(
        _rwkv_post_body,
        grid=(m // tm,),
        in_specs=[_row_spec(tm, RWKV_W)] * 5 + [vec, vec, vec, _const_spec(bd.shape)],
        out_specs=_row_spec(tm, RWKV_W),
        out_shape=jax.ShapeDtypeStruct((m, RWKV_W), BF16),
        compiler_params=_params(("parallel",)),
        name="rwkv_post",
    )(y, r, k, v, g, rk, lnw, lnb, bd)


def _sample_to_chain(x, nb):
    bg = CHAINS // RWKV_HEADS
    g = nb // bg
    x = x.reshape(g, bg, RWKV_HEADS, 2, HALF_N).transpose(0, 4, 3, 1, 2)
    return x.reshape(g, VTILES, 1, SUBLANES, LANES)


def _sample_from_chain(y, nb):
    bg = CHAINS // RWKV_HEADS
    g = nb // bg
    y = y.reshape(g, HALF_N, 2, bg, RWKV_HEADS).transpose(0, 3, 4, 2, 1)
    return y.reshape(nb, RWKV_W)


def _state_to_chains(s, nb):
    bg = CHAINS // RWKV_HEADS
    g = nb // bg
    s = s.reshape(g, bg, RWKV_HEADS, 2, VTILES, SUBLANES, RWKV_N).transpose(0, 4, 6, 5, 3, 1, 2)
    return s.reshape(g, VTILES, RWKV_N, SUBLANES, LANES)


def _state_from_chains(s, nb):
    bg = CHAINS // RWKV_HEADS
    g = nb // bg
    s = s.reshape(g, VTILES, RWKV_N, SUBLANES, 2, bg, RWKV_HEADS).transpose(0, 5, 6, 4, 1, 3, 2)
    return s.reshape(nb, RWKV_HEADS, RWKV_N, RWKV_N)


def _s5_body(u_ref, x0_ref, lre_ref, lim_ref, ldt_ref, bre_ref, bim_ref, cre_ref, cim_ref, d_ref, gw_ref, gbias_ref,
             oin_ref, o_ref, xf_ref, st_ref, ab_ref, bbar_ref, bu_ref, xs_ref, *, nrow, tc):
    del oin_ref
    j = pl.program_id(0)

    @pl.when(j == 0)
    def _():
        dt = jnp.exp(ldt_ref[...])
        lre = lre_ref[...]
        lim = lim_ref[...]
        mag = jnp.exp(lre * dt)
        are = mag * jnp.cos(lim * dt)
        aim = mag * jnp.sin(lim * dt)
        den = lre * lre + lim * lim
        nr = are - 1.0
        cre = (nr * lre + aim * lim) / den
        cim = (aim * lre - nr * lim) / den
        ab_ref[0:1, :] = are
        ab_ref[1:2, :] = aim
        for s in range(S5_SLABS):
            sl = slice(s * S5_SLAB_N, (s + 1) * S5_SLAB_N)
            bre = bre_ref[s]
            bim = bim_ref[s]
            bbar_ref[s, :, 0:S5_SLAB_N] = (bre * cre[:, sl] - bim * cim[:, sl]).astype(BF16)
            bbar_ref[s, :, S5_SLAB_N:2 * S5_SLAB_N] = (bre * cim[:, sl] + bim * cre[:, sl]).astype(BF16)
        st_ref[...] = x0_ref[...]

    for s in range(S5_SLABS):
        bu = _dot(u_ref[:, s * LANES:(s + 1) * LANES].astype(BF16), bbar_ref[s])
        bu_ref[:, s * S5_SLAB_N:(s + 1) * S5_SLAB_N] = bu[:, 0:S5_SLAB_N]
        bu_ref[:, S5_N + s * S5_SLAB_N:S5_N + (s + 1) * S5_SLAB_N] = bu[:, S5_SLAB_N:2 * S5_SLAB_N]
    are = ab_ref[0:1, :]
    aim = ab_ref[1:2, :]

    def token(t, carry):
        rows = pl.ds(pl.multiple_of(t * nrow, nrow), nrow)
        xr = st_ref[:, 0:S5_N]
        xi = st_ref[:, S5_N:2 * S5_N]
        nr = are * xr - aim * xi + bu_ref[rows, 0:S5_N]
        ni = are * xi + aim * xr + bu_ref[rows, S5_N:2 * S5_N]
        st_ref[:, 0:S5_N] = nr
        st_ref[:, S5_N:2 * S5_N] = ni
        xs_ref[rows, 0:S5_N] = nr
        xs_ref[rows, S5_N:2 * S5_N] = ni
        return carry

    lax.fori_loop(0, tc, token, 0)

    ys = []
    for s in range(S5_SLABS):
        sl = slice(s * S5_SLAB_N, (s + 1) * S5_SLAB_N)
        sli = slice(S5_N + s * S5_SLAB_N, S5_N + (s + 1) * S5_SLAB_N)
        ys.append(_dot(xs_ref[:, sl].astype(BF16), cre_ref[s]) - _dot(xs_ref[:, sli].astype(BF16), cim_ref[s]))
    y = jnp.concatenate(ys, axis=1) + d_ref[...] * u_ref[...]
    y = 0.5 * y * (1.0 + jnp.tanh(math.sqrt(2.0 / math.pi) * (y + 0.044715 * (y * y * y))))
    gate = _dot(y.astype(BF16), gw_ref[...]) + gbias_ref[...]
    o_ref[...] = (y * _sigmoid(gate)).astype(o_ref.dtype)

    @pl.when(j == pl.num_programs(0) - 1)
    def _():
        xf_ref[...] = st_ref[...]


def _s5(u, x0, lre, lim, ldt, bre, bim, cre, cim, d, gw, gbias, nrow, t):
    tc = _pick_block(t, 1, 48)
    rows = tc * nrow
    vec = _const_spec((1, S5_N))
    o0 = jnp.zeros((u.shape[0], S5_WIDTH), BF16)
    return pl.pallas_call(
        functools.partial(_s5_body, nrow=nrow, tc=tc),
        grid=(t // tc,),
        in_specs=[_row_spec(rows, S5_WIDTH), _const_spec((nrow, 2 * S5_N)), vec, vec, vec,
                  _const_spec(bre.shape), _const_spec(bim.shape), _const_spec(cre.shape), _const_spec(cim.shape),
                  _const_spec((1, S5_WIDTH)), _const_spec(gw.shape), _const_spec((1, S5_WIDTH)),
                  pl.BlockSpec(memory_space=pl.ANY)],
        out_specs=[_row_spec(rows, S5_WIDTH), pl.BlockSpec((nrow, 2 * S5_N), lambda i: (0, 0))],
        out_shape=[jax.ShapeDtypeStruct((u.shape[0], S5_WIDTH), BF16), jax.ShapeDtypeStruct((nrow, 2 * S5_N), F32)],
        scratch_shapes=[pltpu.VMEM((nrow, 2 * S5_N), F32), pltpu.VMEM((SUBLANES, S5_N), F32),
                        pltpu.VMEM((S5_SLABS, LANES, 2 * S5_SLAB_N), BF16), pltpu.VMEM((rows, 2 * S5_N), F32),
                        pltpu.VMEM((rows, 2 * S5_N), F32)],
        input_output_aliases={12: 0},
        compiler_params=_params(("arbitrary",)),
        name="s5",
    )(u, x0, lre, lim, ldt, bre, bim, cre, cim, d, gw, gbias, o0)


def _merge_body(x_ref, og_ref, or_ref, os_ref, gt_ref, wg_ref, wr_ref, ws_ref, wo_ref, o_ref, *, d):
    m = (_sigmoid(gt_ref[:, 0:d]) * _dot(og_ref[...], wg_ref[...])
         + _sigmoid(gt_ref[:, d:2 * d]) * _dot(or_ref[...], wr_ref[...])
         + _sigmoid(gt_ref[:, 2 * d:3 * d]) * _dot(os_ref[...], ws_ref[...]))
    o_ref[...] = x_ref[...] + _dot(m.astype(BF16), wo_ref[...])


def _merge(x, og, orw, os5, gates, wg, wr, ws, wo):
    m, d = x.shape
    tm = _pick_block(m, 16, 512)
    return pl.pallas_call(
        functools.partial(_merge_body, d=d),
        grid=(m // tm,),
        in_specs=[_row_spec(tm, d), _row_spec(tm, GLA_V), _row_spec(tm, RWKV_W), _row_spec(tm, S5_WIDTH),
                  _row_spec(tm, 3 * d)] + [_const_spec(w.shape) for w in (wg, wr, ws, wo)],
        out_specs=_row_spec(tm, d),
        out_shape=jax.ShapeDtypeStruct((m, d), F32),
        compiler_params=_params(("parallel",)),
        name="merge",
    )(x, og, orw, os5, gates, wg, wr, ws, wo)


def _ffn_body(x_ref, g_ref, w1_ref, w3_ref, w2_ref, o_ref, *, dff, cf):
    x = x_ref[...]
    h = _rms(x, g_ref[...]).astype(BF16)
    acc = x
    for c0 in range(0, dff, cf):
        a = _dot(h, w1_ref[:, c0:c0 + cf])
        b = _dot(h, w3_ref[:, c0:c0 + cf])
        acc = acc + _dot((a * _sigmoid(a) * b).astype(BF16), w2_ref[c0:c0 + cf, :])
    o_ref[...] = acc


def _ffn(x, g, w1, w3, w2):
    m, d = x.shape
    tm = _pick_block(m, SUBLANES, 1024)
    dff = w1.shape[1]
    cf = _pick_block(dff, LANES, 512)
    return pl.pallas_call(
        functools.partial(_ffn_body, dff=dff, cf=cf),
        grid=(m // tm,),
        in_specs=[_row_spec(tm, d), _const_spec((1, d)), _const_spec(w1.shape), _const_spec(w3.shape),
                  _const_spec(w2.shape)],
        out_specs=_row_spec(tm, d),
        out_shape=jax.ShapeDtypeStruct((m, d), F32),
        compiler_params=_params(("parallel",)),
        name="ffn",
    )(x, g, w1, w3, w2)


def _final_norm_body(x_ref, g_ref, o_ref):
    o_ref[...] = _rms(x_ref[...], g_ref[...])


def _final_norm(x, g):
    m, d = x.shape
    tm = _pick_block(m, SUBLANES, 1024)
    return pl.pallas_call(
        _final_norm_body,
        grid=(m // tm,),
        in_specs=[_row_spec(tm, d), _const_spec((1, d))],
        out_specs=_row_spec(tm, d),
        out_shape=jax.ShapeDtypeStruct((m, d), F32),
        compiler_params=_params(("parallel",)),
        name="final_norm",
    )(x, g)


def _block_ones(n, blk, dtype):
    i = jnp.arange(n) // blk
    return (i[:, None] == i[None, :]).astype(dtype)


def _gla_consts():
    c = GLA_SUB
    i = jnp.arange(GLA_TILE)
    same = i[:, None] // c == i[None, :] // c
    tri = ((i[:, None] >= i[None, :]) & same).astype(F32)
    tot = same.astype(F32)
    first = (jnp.arange(GLA_TILE // c)[:, None] * c == i[None, :]).astype(F32)
    segb = (jnp.arange(GLA_QK)[:, None] // GLA_DK == jnp.arange(GLA_V)[None, :] // GLA_DV).astype(BF16)
    sel = (jnp.arange(16)[:, None] // 3 == jnp.arange(3 * GLA_DV)[None, :] // GLA_DV).astype(BF16)
    return (tri, tot, first, segb), sel


def _block_diag_slabs(p, nslab):
    g, a, b = p.shape
    gs = g // nslab
    eye = jnp.eye(gs, dtype=p.dtype)
    p = p.reshape(nslab, gs, a, b)
    return (p[:, :, :, None, :] * eye[None, :, None, :, None]).reshape(nslab, gs * a, gs * b)


def kernel(x_prompt, x_sample, state_gla, state_rwkv, state_rwkv_shift, state_s5_re, state_s5_im, meta_tokens, norm_mix, norm_ffn, w_in, gla_gate_w2, gla_gate_b, gla_norm, rwkv_mu, rwkv_w0, rwkv_w2, rwkv_a0, rwkv_a2, rwkv_g2, rwkv_k_k, rwkv_k_a, rwkv_r_k, rwkv_ln_w, rwkv_ln_b, s5_a_re, s5_a_im, s5_log_dt, s5_b_re, s5_b_im, s5_c_re, s5_c_im, s5_d, s5_glu_w, s5_glu_b, w_br_gla, w_br_rwkv, w_br_s5, w_out, ffn_w1, ffn_w3, ffn_w2, final_norm):
    nb, seq, d = x_prompt.shape
    ns = x_sample.shape[0]
    assert x_sample.shape[1] == 1
    depth = w_in.shape[0]
    t = seq + N_META
    tp = -(-t // LANES) * LANES
    mp = nb * tp
    assert nb * RWKV_HEADS == CHAINS and (ns * RWKV_HEADS) % CHAINS == 0

    meta = jnp.broadcast_to(meta_tokens.astype(x_prompt.dtype)[None], (nb, N_META, d))
    xp = jnp.concatenate([meta, x_prompt, jnp.zeros((nb, tp - t, d), x_prompt.dtype)], axis=1).reshape(mp, d)
    xs = x_sample.reshape(ns, d)

    gla_unroll = 3 if t % (3 * GLA_SUB) == 0 else 1
    gla_consts, sel = _gla_consts()
    bd = _block_ones(RWKV_W, RWKV_N, BF16)
    o_q, o_z = 0, 2 * GLA_QK + 2 * GLA_V
    o_r = o_z + GLA_GATE_RANK
    o_s = o_r + RWKV_COLS
    o_t = o_s + S5_WIDTH

    new_p = [[] for _ in range(5)]
    new_s = [[] for _ in range(5)]
    for l in range(depth):
        wl = w_in[l]
        wg = jnp.concatenate([wl[:, o_q:o_r], jnp.zeros((d, GLA_ZPAD - GLA_GATE_RANK), F32)], axis=1).astype(BF16)
        w_groups = (wg, wl[:, o_r:o_s].astype(BF16), wl[:, o_s:o_t].astype(BF16), wl[:, o_t:].astype(BF16))
        pg_p, pr_p, ps_p, gt_p = _inproj(xp, norm_mix[l][None], *w_groups)
        pg_s, pr_s, ps_s, gt_s = _inproj(xs, norm_mix[l][None], *w_groups)

        w2p = jnp.concatenate([gla_gate_w2[l], jnp.zeros((GLA_ZPAD - GLA_GATE_RANK, GLA_QK), F32)], axis=0).astype(BF16)
        gb = gla_gate_b[l][None]
        gn = gla_norm[l][None]
        og_p, gla_p = _gla_state(*_gla_intra(pg_p, w2p, gb, gn, gla_consts), nb, t, tp, gla_unroll)
        gla_p = jnp.swapaxes(gla_p[:, :, :, :GLA_DK], 2, 3)
        og_s, gla_s = _gla_sample(pg_s, state_gla[l].reshape(ns, GLA_QK, GLA_DV), w2p, gb, gn, sel)
        new_p[0].append(gla_p.reshape(nb, GLA_HEADS, GLA_DK, GLA_DV))
        new_s[0].append(gla_s.reshape(ns, GLA_HEADS, GLA_DK, GLA_DV))

        wlow = jnp.zeros((RWKV_LOW, 3 * RWKV_W), F32)
        wlow = wlow.at[0:RWKV_W_RANK, 0:RWKV_W].set(rwkv_w2[l])
        wlow = wlow.at[RWKV_W_RANK:RWKV_W_RANK + RWKV_A_RANK, RWKV_W:2 * RWKV_W].set(rwkv_a2[l])
        wlow = wlow.at[RWKV_W_RANK + RWKV_A_RANK:, 2 * RWKV_W:].set(rwkv_g2[l])
        prep_consts = (rwkv_mu[l][None], rwkv_w0[l][None], rwkv_a0[l][None], rwkv_k_k[l][None], rwkv_k_a[l][None],
                       wlow.astype(BF16), bd)
        post_consts = (rwkv_r_k[l].reshape(1, RWKV_W), rwkv_ln_w[l][None], rwkv_ln_b[l][None], bd)
        r_, w_, k_, v_, kk_, kka_, g_ = _rwkv_prep(pr_p, None, prep_consts, nb, tp)
        chain = _to_chain((w_, kk_, kka_), nb, tp) + _to_chain((k_, r_, v_), nb, tp)
        yt, sf = _rwkv_scan(*chain, _state_to_chains(jnp.zeros((nb, RWKV_HEADS, RWKV_N, RWKV_N), F32), nb), t)
        orw_p = _rwkv_post(_from_chain(yt.reshape(VTILES, tp, SUBLANES * LANES), nb, tp), r_, k_, v_, g_, *post_consts)
        new_p[1].append(_state_from_chains(sf, nb))
        new_p[2].append(pr_p.reshape(nb, tp, RWKV_COLS)[:, t - 1])
        r_, w_, k_, v_, kk_, kka_, g_ = _rwkv_prep(pr_s, state_rwkv_shift[l], prep_consts, ns, 1)
        chain = [_sample_to_chain(a, ns) for a in (w_, kk_, kka_, k_, r_, v_)]
        yt, sf = _rwkv_scan(*chain, _state_to_chains(state_rwkv[l], ns), 1)
        orw_s = _rwkv_post(_sample_from_chain(yt, ns), r_, k_, v_, g_, *post_consts)
        new_s[1].append(_state_from_chains(sf, ns))
        new_s[2].append(pr_s)

        lre = s5_a_re[l].reshape(1, S5_N)
        lim = s5_a_im[l].reshape(1, S5_N)
        ldt = jnp.broadcast_to(s5_log_dt[l][:, None], (S5_GROUPS, S5_STATE)).reshape(1, S5_N)
        bre = _block_diag_slabs(jnp.swapaxes(s5_b_re[l], 1, 2), S5_SLABS)
        bim = _block_diag_slabs(jnp.swapaxes(s5_b_im[l], 1, 2), S5_SLABS)
        cre = _block_diag_slabs(jnp.swapaxes(s5_c_re[l], 1, 2), S5_SLABS).astype(BF16)
        cim = _block_diag_slabs(jnp.swapaxes(s5_c_im[l], 1, 2), S5_SLABS).astype(BF16)
        s5_args = (lre, lim, ldt, bre, bim, cre, cim, s5_d[l][None], s5_glu_w[l].astype(BF16), s5_glu_b[l][None])
        u_p = ps_p.reshape(nb, tp, S5_WIDTH).transpose(1, 0, 2).reshape(mp, S5_WIDTH)
        os_p, xf_p = _s5(u_p, jnp.zeros((nb, 2 * S5_N), F32), *s5_args, nb, t)
        os_p = os_p.reshape(tp, nb, S5_WIDTH).transpose(1, 0, 2).reshape(mp, S5_WIDTH)
        x0_s = jnp.concatenate([state_s5_re[l].reshape(ns, S5_N), state_s5_im[l].reshape(ns, S5_N)], axis=1)
        os_s, xf_s = _s5(ps_s, x0_s, *s5_args, ns, 1)
        new_p[3].append(xf_p[:, :S5_N].reshape(nb, S5_GROUPS, S5_STATE))
        new_p[4].append(xf_p[:, S5_N:].reshape(nb, S5_GROUPS, S5_STATE))
        new_s[3].append(xf_s[:, :S5_N].reshape(ns, S5_GROUPS, S5_STATE))
        new_s[4].append(xf_s[:, S5_N:].reshape(ns, S5_GROUPS, S5_STATE))

        w_merge = (w_br_gla[l].astype(BF16), w_br_rwkv[l].astype(BF16), w_br_s5[l].astype(BF16), w_out[l].astype(BF16))
        w_ffn = (norm_ffn[l][None], ffn_w1[l].astype(BF16), ffn_w3[l].astype(BF16), ffn_w2[l].astype(BF16))
        xp = _ffn(_merge(xp, og_p, orw_p, os_p, gt_p, *w_merge), *w_ffn)
        xs = _ffn(_merge(xs, og_s, orw_s, os_s, gt_s, *w_merge), *w_ffn)

    y_prompt = _final_norm(xp, final_norm[None]).reshape(nb, tp, d)[:, N_META:t]
    y_sample = _final_norm(xs, final_norm[None]).reshape(ns, 1, d)
    outs_p = [jnp.stack(a, axis=0) for a in new_p]
    outs_s = [jnp.stack(a, axis=0) for a in new_s]
    return (y_prompt, y_sample, *outs_p, *outs_s)
```

```python
import functools
import math

import jax
import jax.numpy as jnp
from jax import lax
from jax.experimental import pallas as pl
from jax.experimental.pallas import tpu as pltpu

F32 = jnp.float32
BF16 = jnp.bfloat16
HIGHEST = lax.Precision.HIGHEST

N_META = 16
NORM_EPS = 1e-6
GLA_HEADS = 4
GLA_DK = 64
GLA_DV = 128
GLA_QK = GLA_HEADS * GLA_DK
GLA_V = GLA_HEADS * GLA_DV
GLA_GATE_RANK = 16
GLA_TAU = 16.0
GLA_SUB = 16
RWKV_HEADS = 8
RWKV_N = 64
RWKV_W = RWKV_HEADS * RWKV_N
RWKV_W_RANK = 64
RWKV_A_RANK = 64
RWKV_G_RANK = 128
RWKV_LOW = RWKV_W_RANK + RWKV_A_RANK + RWKV_G_RANK
RWKV_COLS = 3 * RWKV_W + RWKV_LOW
RWKV_GN_EPS = 64e-5
S5_GROUP = 16
S5_GROUPS = 32
S5_WIDTH = S5_GROUPS * S5_GROUP
S5_STATE = 64
S5_N = S5_GROUPS * S5_STATE

LANES = 128
SUBLANES = 8
CHAINS = LANES // 2
HALF_N = RWKV_N // 2
VTILES = HALF_N // SUBLANES
S5_SLABS = S5_WIDTH // LANES
S5_SLAB_N = S5_N // S5_SLABS
GLA_ZPAD = LANES
GLA_COLS = 2 * GLA_QK + 2 * GLA_V + GLA_ZPAD
VMEM_LIMIT = 56 << 20


def _params(sem):
    return pltpu.CompilerParams(dimension_semantics=sem, vmem_limit_bytes=VMEM_LIMIT)


def _const_spec(shape):
    nd = len(shape)
    return pl.BlockSpec(shape, lambda *_: (0,) * nd, pipeline_mode=pl.Buffered(1))


def _row_spec(tm, cols):
    return pl.BlockSpec((tm, cols), lambda i: (i, 0))


def _pick_block(n, mult, cap):
    best = None
    for d in range(mult, min(n, cap) + 1, mult):
        if n % d == 0:
            best = d
    assert best is not None, (n, mult, cap)
    return best


def _sigmoid(x):
    return 1.0 / (1.0 + jnp.exp(-x))


def _softplus(x):
    return jnp.maximum(x, 0.0) + jnp.log1p(jnp.exp(-jnp.abs(x)))


def _rms(x, g):
    return x * lax.rsqrt(jnp.mean(x * x, axis=-1, keepdims=True) + NORM_EPS) * g


def _dot(a, b):
    return jnp.dot(a, b, preferred_element_type=F32)


def _dot_t0(a, b):
    return lax.dot_general(a, b, (((0,), (0,)), ((), ())), preferred_element_type=F32)


def _split3_rows(x):
    hi = x.astype(BF16).astype(F32)
    r1 = x - hi
    mid = r1.astype(BF16).astype(F32)
    lo = (r1 - mid).astype(BF16).astype(F32)
    return [hi, mid, lo]


def _inproj_body(x_ref, g_ref, *refs):
    n = len(refs) // 2
    h = _rms(x_ref[...], g_ref[...]).astype(BF16)
    for w_ref, o_ref in zip(refs[:n], refs[n:]):
        o_ref[...] = _dot(h, w_ref[...])


def _inproj(x, g, ws):
    m, d = x.shape
    tm = _pick_block(m, SUBLANES, 512)
    widths = [w.shape[1] for w in ws]
    return pl.pallas_call(
        _inproj_body,
        grid=(m // tm,),
        in_specs=[_row_spec(tm, d), _const_spec((1, d))] + [_const_spec(w.shape) for w in ws],
        out_specs=[_row_spec(tm, c) for c in widths],
        out_shape=[jax.ShapeDtypeStruct((m, c), F32) for c in widths],
        compiler_params=_params(("parallel",)),
        name="inproj",
    )(x, g, *ws)


def _log_sigmoid(x):
    return jnp.minimum(x, 0.0) - jnp.log1p(jnp.exp(-jnp.abs(x)))


def _gla_finish(o, g, gn):
    parts = []
    for h in range(GLA_HEADS):
        oh = o[:, h * GLA_DV:(h + 1) * GLA_DV]
        parts.append(oh * lax.rsqrt(jnp.mean(oh * oh, axis=-1, keepdims=True) + NORM_EPS))
    on = jnp.concatenate(parts, axis=1)
    return on * gn * (g * _sigmoid(g))


def _gla_log_decay(z, w2_ref, gb_ref):
    zz = _dot(z.astype(BF16), w2_ref[...]) + gb_ref[...]
    return _log_sigmoid(zz) * (1.0 / GLA_TAU)


GLA_TILE = LANES


def _pad_heads(x):
    zero = jnp.zeros((x.shape[0], LANES - GLA_DK), x.dtype)
    parts = []
    for h in range(GLA_HEADS):
        parts += [x[:, h * GLA_DK:(h + 1) * GLA_DK], zero]
    return jnp.concatenate(parts, axis=1)


def _gla_intra_body(p_ref, w2_ref, gb_ref, gn_ref, tri_ref, tot_ref, first_ref, segb_ref,
                    oi_ref, qe_ref, kd_ref, vb_ref, gate_ref, ebl_ref):
    c = GLA_SUB
    q = p_ref[:, 0:GLA_QK]
    k = p_ref[:, GLA_QK:2 * GLA_QK]
    v = p_ref[:, 2 * GLA_QK:2 * GLA_QK + GLA_V]
    g = p_ref[:, 2 * GLA_QK + GLA_V:2 * GLA_QK + 2 * GLA_V]
    z = p_ref[:, 2 * GLA_QK + 2 * GLA_V:GLA_COLS]
    la = _gla_log_decay(z, w2_ref, gb_ref)
    bc = jnp.dot(tri_ref[...], la, precision=HIGHEST, preferred_element_type=F32)
    bl = jnp.dot(tot_ref[...], la, precision=HIGHEST, preferred_element_type=F32)
    qs = q * (GLA_DK ** -0.5)
    qe_ref[...] = _pad_heads(qs * jnp.exp(bc)).astype(BF16)
    kd_ref[...] = _pad_heads(k * jnp.exp(bl - bc)).astype(BF16)
    ebl_ref[...] = jnp.dot(first_ref[...], _pad_heads(jnp.exp(bl)), precision=HIGHEST, preferred_element_type=F32)
    vb_ref[...] = v.astype(BF16)
    gate_ref[...] = gn_ref[...] * (g * _sigmoid(g))
    row_id = lax.broadcasted_iota(jnp.int32, (c, GLA_QK), 0)
    nchunk = p_ref.shape[0] // c
    per_dot = 4
    for c0 in range(0, nchunk, per_dot):
        terms = []
        for u in range(c0, c0 + per_dot):
            sl = slice(u * c, (u + 1) * c)
            for jj in range(c):
                e = jnp.exp(jnp.minimum(bc[sl] - bc[u * c + jj:u * c + jj + 1, :], 0.0))
                terms.append(jnp.where(row_id >= jj, qs[sl] * k[u * c + jj:u * c + jj + 1, :] * e, 0.0))
        pw = _dot(jnp.concatenate(terms, axis=0).astype(BF16), segb_ref[...])
        for i, u in enumerate(range(c0, c0 + per_dot)):
            base = i * c * c
            acc = pw[base:base + c] * v[u * c:u * c + 1, :]
            for jj in range(1, c):
                acc = acc + pw[base + jj * c:base + (jj + 1) * c] * v[u * c + jj:u * c + jj + 1, :]
            oi_ref[u * c:(u + 1) * c, :] = acc


def _gla_intra(pg, w2p, gb, gn, consts):
    m = pg.shape[0]
    tm = GLA_TILE
    assert m % tm == 0
    tri, tot, first, segb = consts
    nck = tm // GLA_SUB
    wide = GLA_HEADS * LANES
    return pl.pallas_call(
        _gla_intra_body,
        grid=(m // tm,),
        in_specs=[_row_spec(tm, GLA_COLS), _const_spec(w2p.shape), _const_spec(gb.shape), _const_spec(gn.shape),
                  _const_spec(tri.shape), _const_spec(tot.shape), _const_spec(first.shape), _const_spec(segb.shape)],
        out_specs=[_row_spec(tm, GLA_V), _row_spec(tm, wide), _row_spec(tm, wide), _row_spec(tm, GLA_V),
                   _row_spec(tm, GLA_V), _row_spec(nck, wide)],
        out_shape=[jax.ShapeDtypeStruct((m, GLA_V), F32), jax.ShapeDtypeStruct((m, wide), BF16),
                   jax.ShapeDtypeStruct((m, wide), BF16), jax.ShapeDtypeStruct((m, GLA_V), BF16),
                   jax.ShapeDtypeStruct((m, GLA_V), F32), jax.ShapeDtypeStruct((m // GLA_SUB, wide), F32)],
        compiler_params=_params(("parallel",)),
        name="gla_intra",
    )(pg, w2p, gb, gn, tri, tot, first, segb)


def _gla_state_body(oi_ref, qe_ref, kd_ref, vb_ref, gate_ref, ebl_ref, o_ref, sf_ref, st_ref, *, ngroup, unroll):
    c = GLA_SUB
    gr = c * unroll

    st_ref[...] = jnp.zeros_like(st_ref)

    def group(gi, carry):
        rows = pl.ds(pl.multiple_of(gi * gr, gr), gr)
        qe = qe_ref[rows, :]
        kd = kd_ref[rows, :]
        vb = vb_ref[rows, :]
        st = [st_ref[h] for h in range(GLA_HEADS)]
        outs = []
        for u in range(unroll):
            sl = slice(u * c, (u + 1) * c)
            dec = ebl_ref[gi * unroll + u]
            parts = []
            for h in range(GLA_HEADS):
                hs = slice(h * LANES, (h + 1) * LANES)
                parts.append(lax.dot_general(qe[sl, hs], st[h].astype(BF16), (((1,), (1,)), ((), ())),
                                             preferred_element_type=F32))
                st[h] = st[h] * dec[:, hs] + _dot_t0(vb[sl, hs], kd[sl, hs])
            outs.append(jnp.concatenate(parts, axis=1))
        o = (outs[0] if unroll == 1 else jnp.concatenate(outs, axis=0)) + oi_ref[rows, :]
        normed = []
        for h in range(GLA_HEADS):
            oh = o[:, h * GLA_DV:(h + 1) * GLA_DV]
            normed.append(oh * lax.rsqrt(jnp.mean(oh * oh, axis=-1, keepdims=True) + NORM_EPS))
        o_ref[rows, :] = (jnp.concatenate(normed, axis=1) * gate_ref[rows, :]).astype(o_ref.dtype)
        for h in range(GLA_HEADS):
            st_ref[h] = st[h]
        return carry

    lax.fori_loop(0, ngroup, group, 0)
    t = ngroup * gr
    if t < o_ref.shape[0]:
        o_ref[t:, :] = jnp.zeros((o_ref.shape[0] - t, GLA_V), o_ref.dtype)
    sf_ref[0] = st_ref[...]


def _gla_state(oi, qe, kd, vb, gate, ebl, nb, t, tp, unroll):
    gr = GLA_SUB * unroll
    assert t % gr == 0
    wide = GLA_HEADS * LANES
    nck = tp // GLA_SUB
    seq = lambda cols: pl.BlockSpec((tp, cols), lambda b: (b, 0))
    return pl.pallas_call(
        functools.partial(_gla_state_body, ngroup=t // gr, unroll=unroll),
        grid=(nb,),
        in_specs=[seq(GLA_V), seq(wide), seq(wide), seq(GLA_V), seq(GLA_V),
                  pl.BlockSpec((nck, 1, wide), lambda b: (b, 0, 0))],
        out_specs=[seq(GLA_V), pl.BlockSpec((1, GLA_HEADS, GLA_DV, LANES), lambda b: (b, 0, 0, 0))],
        out_shape=[jax.ShapeDtypeStruct((nb * tp, GLA_V), BF16),
                   jax.ShapeDtypeStruct((nb, GLA_HEADS, GLA_DV, LANES), F32)],
        scratch_shapes=[pltpu.VMEM((GLA_HEADS, GLA_DV, LANES), F32)],
        compiler_params=_params(("parallel",)),
        name="gla_state",
    )(oi, qe, kd, vb, gate, ebl.reshape(nb * nck, 1, wide))


def _gla_sample_body(p_ref, s0_ref, w2_ref, gb_ref, gn_ref, sel_ref, o_ref, s_ref, *, nrow):
    q = p_ref[:, 0:GLA_QK]
    k = p_ref[:, GLA_QK:2 * GLA_QK]
    v = p_ref[:, 2 * GLA_QK:2 * GLA_QK + GLA_V]
    g = p_ref[:, 2 * GLA_QK + GLA_V:2 * GLA_QK + 2 * GLA_V]
    z = p_ref[:, 2 * GLA_QK + 2 * GLA_V:GLA_COLS]
    a = jnp.exp(_gla_log_decay(z, w2_ref, gb_ref))
    qs = q * (GLA_DK ** -0.5)
    out_rows = []
    for i in range(nrow):
        pieces = _split3_rows(a[i:i + 1]) + _split3_rows(k[i:i + 1]) + _split3_rows(qs[i:i + 1])
        ef = jnp.concatenate(pieces + [jnp.zeros((16 - 9, GLA_QK), F32)], axis=0)
        cb = _dot_t0(ef.astype(BF16), sel_ref[...])
        ab = cb[:, 0:GLA_DV]
        kb = cb[:, GLA_DV:2 * GLA_DV]
        qb = cb[:, 2 * GLA_DV:3 * GLA_DV]
        vb = jnp.concatenate(
            [jnp.broadcast_to(v[i:i + 1, h * GLA_DV:(h + 1) * GLA_DV], (GLA_DK, GLA_DV)) for h in range(GLA_HEADS)],
            axis=0)
        s = ab * s0_ref[i] + kb * vb
        s_ref[i] = s
        op = qb * s
        out_rows.append(jnp.concatenate(
            [jnp.sum(op[h * GLA_DK:(h + 1) * GLA_DK], axis=0, keepdims=True) for h in range(GLA_HEADS)], axis=1))
    o = jnp.concatenate(out_rows, axis=0)
    o_ref[...] = _gla_finish(o, g, gn_ref[...]).astype(o_ref.dtype)


def _gla_sample(pg, s0, w2p, gb, gn, sel):
    ns = pg.shape[0]
    nrow = 16
    assert ns % nrow == 0
    return pl.pallas_call(
        functools.partial(_gla_sample_body, nrow=nrow),
        grid=(ns // nrow,),
        in_specs=[_row_spec(nrow, GLA_COLS),
                  pl.BlockSpec((nrow, GLA_QK, GLA_DV), lambda i: (i, 0, 0)),
                  _const_spec(w2p.shape), _const_spec(gb.shape), _const_spec(gn.shape), _const_spec(sel.shape)],
        out_specs=[_row_spec(nrow, GLA_V), pl.BlockSpec((nrow, GLA_QK, GLA_DV), lambda i: (i, 0, 0))],
        out_shape=[jax.ShapeDtypeStruct((ns, GLA_V), BF16), jax.ShapeDtypeStruct((ns, GLA_QK, GLA_DV), F32)],
        compiler_params=_params(("parallel",)),
        name="gla_sample",
    )(pg, s0, w2p, gb, gn, sel)


def _segsum(x, bd):
    hi = x.astype(BF16)
    lo = (x - hi.astype(F32)).astype(BF16)
    return _dot(hi, bd) + _dot(lo, bd)


def _rwkv_prep_math(p, prev, mu_ref, w0_ref, a0_ref, kkw_ref, kaw_ref, wlow_ref, bd_ref):
    xs = p + (prev - p) * mu_ref[...]
    r = xs[:, 0:RWKV_W]
    k = xs[:, RWKV_W:2 * RWKV_W]
    v = xs[:, 2 * RWKV_W:3 * RWKV_W]
    zl = xs[:, 3 * RWKV_W:RWKV_COLS]
    lane = lax.broadcasted_iota(jnp.int32, zl.shape, 1)
    zt = jnp.where(lane < RWKV_W_RANK, jnp.tanh(zl),
                   jnp.where(lane < RWKV_W_RANK + RWKV_A_RANK, zl, _sigmoid(zl)))
    low = _dot(zt.astype(BF16), wlow_ref[...])
    w_log = -_softplus(-(w0_ref[...] + low[:, 0:RWKV_W])) - 0.5
    a = _sigmoid(a0_ref[...] + low[:, RWKV_W:2 * RWKV_W])
    kk = k * kkw_ref[...]
    kk = kk * lax.rsqrt(_segsum(kk * kk, bd_ref[...]) + 1e-12)
    return (r, jnp.exp(-jnp.exp(w_log)), k * (1.0 + (a - 1.0) * kaw_ref[...]), v, kk, kk * a,
            low[:, 2 * RWKV_W:3 * RWKV_W])


PREP_SEQS = 4


def _rwkv_prep_seq_body(p_ref, *rest):
    consts, outs, carry_ref = rest[:7], rest[7:17], rest[17]
    nat, tr = outs[:4], outs[4:]

    @pl.when(pl.program_id(1) == 0)
    def _():
        carry_ref[...] = jnp.zeros_like(carry_ref)

    row = lax.broadcasted_iota(jnp.int32, p_ref.shape[1:], 0)
    for b in range(p_ref.shape[0]):
        p = p_ref[b]
        prev = jnp.where(row == 0, carry_ref[b:b + 1, :], pltpu.roll(p, 1, axis=0))
        carry_ref[b:b + 1, :] = p[p.shape[0] - 1:, :]
        r, w, k, v, kk, kka, g = _rwkv_prep_math(p, prev, *consts)
        for ref, val in zip(nat, (r, k, v, g)):
            ref[b] = val
        for ref, val in zip(tr, (w, kk, kka, k, r, v)):
            ref[b] = val.T


def _rwkv_prep_step_body(p_ref, pp_ref, *rest):
    for ref, val in zip(rest[7:], _rwkv_prep_math(p_ref[...], pp_ref[...], *rest[:7])):
        ref[...] = val


def _rwkv_prep_consts_specs(consts):
    vec = _const_spec((1, RWKV_W))
    return [_const_spec((1, RWKV_COLS)), vec, vec, vec, vec, _const_spec(consts[5].shape), _const_spec(consts[6].shape)]


def _rwkv_prep_seq(p, consts, nb, tp):
    tb = LANES
    ng = PREP_SEQS
    assert nb % ng == 0 and tp % tb == 0
    nat = pl.BlockSpec((ng, tb, RWKV_W), lambda i, j: (i, j, 0))
    tr = pl.BlockSpec((ng, RWKV_W, tb), lambda i, j: (i, 0, j))
    outs = pl.pallas_call(
        _rwkv_prep_seq_body,
        grid=(nb // ng, tp // tb),
        in_specs=[pl.BlockSpec((ng, tb, RWKV_COLS), lambda i, j: (i, j, 0))] + _rwkv_prep_consts_specs(consts),
        out_specs=[nat] * 4 + [tr] * 6,
        out_shape=[jax.ShapeDtypeStruct((nb, tp, RWKV_W), F32)] * 4 + [jax.ShapeDtypeStruct((nb, RWKV_W, tp), F32)] * 6,
        scratch_shapes=[pltpu.VMEM((ng, RWKV_COLS), F32)],
        compiler_params=_params(("parallel", "arbitrary")),
        name="rwkv_prep_seq",
    )(p.reshape(nb, tp, RWKV_COLS), *consts)
    return [o.reshape(nb * tp, RWKV_W) for o in outs[:4]], list(outs[4:])


def _rwkv_prep_step(p, prev, consts):
    nb = p.shape[0]
    tm = _pick_block(nb, SUBLANES, 256)
    return pl.pallas_call(
        _rwkv_prep_step_body,
        grid=(nb // tm,),
        in_specs=[_row_spec(tm, RWKV_COLS), _row_spec(tm, RWKV_COLS)] + _rwkv_prep_consts_specs(consts),
        out_specs=[_row_spec(tm, RWKV_W)] * 7,
        out_shape=[jax.ShapeDtypeStruct((nb, RWKV_W), F32)] * 7,
        compiler_params=_params(("parallel",)),
        name="rwkv_prep_step",
    )(p, prev, *consts)


def _to_chain_body(*refs, nb, narr):
    x_refs, o_refs = refs[:narr], refs[narr:]
    tc = x_refs[0].shape[2]

    def chains(x_ref, n):
        return [x_ref[b, pl.ds(n, RWKV_HEADS, stride=RWKV_N), :] for b in range(nb)]

    for i in range(narr):
        for r in range(HALF_N):
            both = jnp.concatenate(chains(x_refs[i], r) + chains(x_refs[i], HALF_N + r), axis=0)
            o_refs[i][pl.ds(r, tc, stride=HALF_N), :] = both.T


def _to_chain(xs, nb, tp):
    assert nb * RWKV_HEADS == CHAINS and tp % LANES == 0
    tc = LANES
    narr = len(xs)
    outs = pl.pallas_call(
        functools.partial(_to_chain_body, nb=nb, narr=narr),
        grid=(tp // tc,),
        in_specs=[pl.BlockSpec((nb, RWKV_W, tc), lambda j: (0, 0, j))] * narr,
        out_specs=[pl.BlockSpec((tc * HALF_N, LANES), lambda j: (j, 0))] * narr,
        out_shape=[jax.ShapeDtypeStruct((tp * HALF_N, LANES), F32)] * narr,
        compiler_params=_params(("parallel",)),
        name="rwkv_to_chain",
    )(*xs)
    return [o.reshape(1, tp, HALF_N, LANES) for o in outs]


def _from_chain_body(y_ref, o_ref, z_ref, *, nb):
    tc = o_ref.shape[1]
    for r in range(HALF_N):
        both = y_ref[pl.ds(r, tc, stride=HALF_N), :].T
        z_ref[pl.ds(r, CHAINS, stride=RWKV_N), :] = both[0:CHAINS]
        z_ref[pl.ds(HALF_N + r, CHAINS, stride=RWKV_N), :] = both[CHAINS:LANES]
    for b in range(nb):
        o_ref[b] = z_ref[b * RWKV_W:(b + 1) * RWKV_W, :].T


def _from_chain(y, nb, tp):
    tc = LANES
    out = pl.pallas_call(
        functools.partial(_from_chain_body, nb=nb),
        grid=(tp // tc,),
        in_specs=[pl.BlockSpec((tc * HALF_N, LANES), lambda j: (j, 0))],
        out_specs=pl.BlockSpec((nb, tc, RWKV_W), lambda j: (0, j, 0)),
        out_shape=jax.ShapeDtypeStruct((nb, tp, RWKV_W), F32),
        scratch_shapes=[pltpu.VMEM((nb * RWKV_W, tc), F32)],
        compiler_params=_params(("parallel",)),
        name="rwkv_from_chain",
    )(y)
    return out.reshape(nb * tp, RWKV_W)


def _rwkv_scan_body(w_ref, kk_ref, kka_ref, k_ref, r_ref, v_ref, s0_ref, yin_ref, y_ref, sf_ref, s_ref, u_ref, *, tc):
    del yin_ref
    j = pl.program_id(1)
    key_refs = (w_ref, kk_ref, kka_ref, k_ref, r_ref)
    low_half = lax.broadcasted_iota(jnp.int32, (HALF_N, LANES), 1) < CHAINS

    def spread(t, slot):
        for i, ref in enumerate(key_refs):
            x = ref[0, t]
            sw = pltpu.roll(x, CHAINS, axis=1)
            u_ref[slot, i, 0:HALF_N, :] = jnp.where(low_half, x, sw)
            u_ref[slot, i, HALF_N:RWKV_N, :] = jnp.where(low_half, sw, x)

    @pl.when(j == 0)
    def _():
        s_ref[...] = s0_ref[0]

    spread(0, 0)

    nsum = 4

    def total(parts):
        return (parts[0] + parts[1]) + (parts[2] + parts[3])

    def token(t, carry):
        slot = t % 2
        acc = [[None] * nsum for _ in range(VTILES)]
        for key in range(RWKV_N):
            kk = u_ref[slot, 1, key:key + 1, :]
            for i in range(VTILES):
                term = s_ref[i, key] * kk
                acc[i][key % nsum] = term if key < nsum else acc[i][key % nsum] + term
        sk = [total(a) for a in acc]
        vv = [v_ref[0, t, i * SUBLANES:(i + 1) * SUBLANES, :] for i in range(VTILES)]
        yacc = [[None] * nsum for _ in range(VTILES)]
        for key in range(RWKV_N):
            w = u_ref[slot, 0, key:key + 1, :]
            kka = u_ref[slot, 2, key:key + 1, :]
            k = u_ref[slot, 3, key:key + 1, :]
            r = u_ref[slot, 4, key:key + 1, :]
            for i in range(VTILES):
                sn = s_ref[i, key] * w - sk[i] * kka + vv[i] * k
                s_ref[i, key] = sn
                term = sn * r
                yacc[i][key % nsum] = term if key < nsum else yacc[i][key % nsum] + term
        for i in range(VTILES):
            y_ref[0, t, i * SUBLANES:(i + 1) * SUBLANES, :] = total(yacc[i])
        spread(jnp.minimum(t + 1, tc - 1), 1 - slot)
        return carry

    lax.fori_loop(0, tc, token, 0)

    @pl.when(j == pl.num_programs(1) - 1)
    def _():
        sf_ref[0] = s_ref[...]


def _rwkv_scan(w, kk, kka, k, r, v, s0, t):
    g, tp = w.shape[0], w.shape[1]
    tc = _pick_block(t, 1, 96)
    tok = pl.BlockSpec((1, tc, HALF_N, LANES), lambda i, j: (i, j, 0, 0))
    st = pl.BlockSpec((1, VTILES, RWKV_N, SUBLANES, LANES), lambda i, j: (i, 0, 0, 0, 0))
    y0 = jnp.zeros((g, tp, HALF_N, LANES), F32)
    return pl.pallas_call(
        functools.partial(_rwkv_scan_body, tc=tc),
        grid=(g, t // tc),
        in_specs=[tok] * 6 + [st, pl.BlockSpec(memory_space=pl.ANY)],
        out_specs=[tok, st],
        out_shape=[jax.ShapeDtypeStruct((g, tp, HALF_N, LANES), F32),
                   jax.ShapeDtypeStruct((g, VTILES, RWKV_N, SUBLANES, LANES), F32)],
        scratch_shapes=[pltpu.VMEM((VTILES, RWKV_N, SUBLANES, LANES), F32),
                        pltpu.VMEM((2, 5, RWKV_N, LANES), F32)],
        input_output_aliases={7: 0},
        compiler_params=_params(("parallel", "arbitrary")),
        name="rwkv_scan",
    )(w, kk, kka, k, r, v, s0, y0)


def _rwkv_post_body(y_ref, r_ref, k_ref, v_ref, g_ref, rk_ref, lnw_ref, lnb_ref, bd_ref, o_ref):
    y = y_ref[...]
    bd = bd_ref[...]
    inv_n = 1.0 / RWKV_N
    yc = y - _segsum(y, bd) * inv_n
    var = _segsum(yc * yc, bd) * inv_n
    yn = yc * lax.rsqrt(var + RWKV_GN_EPS) * lnw_ref[...] + lnb_ref[...]
    bonus = _segsum(r_ref[...] * k_ref[...] * rk_ref[...], bd) * v_ref[...]
    o_ref[...] = ((yn + bonus) * g_ref[...]).astype(o_ref.dtype)


def _rwkv_post(y, r, k, v, g, rk, lnw, lnb, bd):
    m = y.shape[0]
    tm = _pick_block(m, 16, 512)
    vec = _const_spec((1, RWKV_W))
    return pl.pallas_call(
        _rwkv_post_body,
        grid=(m // tm,),
        in_specs=[_row_spec(tm, RWKV_W)] * 5 + [vec, vec, vec, _const_spec(bd.shape)],
        out_specs=_row_spec(tm, RWKV_W),
        out_shape=jax.ShapeDtypeStruct((m, RWKV_W), BF16),
        compiler_params=_params(("parallel",)),
        name="rwkv_post",
    )(y, r, k, v, g, rk, lnw, lnb, bd)


def _sample_to_chain(x, nb):
    bg = CHAINS // RWKV_HEADS
    g = nb // bg
    x = x.reshape(g, bg, RWKV_HEADS, 2, HALF_N).transpose(0, 4, 3, 1, 2)
    return x.reshape(g, 1, HALF_N, LANES)


def _sample_from_chain(y, nb):
    bg = CHAINS // RWKV_HEADS
    g = nb // bg
    y = y.reshape(g, HALF_N, 2, bg, RWKV_HEADS).transpose(0, 3, 4, 2, 1)
    return y.reshape(nb, RWKV_W)


def _state_to_chains(s, nb):
    bg = CHAINS // RWKV_HEADS
    g = nb // bg
    s = s.reshape(g, bg, RWKV_HEADS, 2, VTILES, SUBLANES, RWKV_N).transpose(0, 4, 6, 5, 3, 1, 2)
    return s.reshape(g, VTILES, RWKV_N, SUBLANES, LANES)


def _state_from_chains(s, nb):
    bg = CHAINS // RWKV_HEADS
    g = nb // bg
    s = s.reshape(g, VTILES, RWKV_N, SUBLANES, 2, bg, RWKV_HEADS).transpose(0, 5, 6, 4, 1, 3, 2)
    return s.reshape(nb, RWKV_HEADS, RWKV_N, RWKV_N)


def _s5_body(u_ref, x0_ref, lre_ref, lim_ref, ldt_ref, bre_ref, bim_ref, cre_ref, cim_ref, d_ref, gw_ref, gbias_ref,
             oin_ref, o_ref, xf_ref, st_ref, ab_ref, bbar_ref, bu_ref, xs_ref, *, nrow, tc):
    del oin_ref
    j = pl.program_id(0)

    @pl.when(j == 0)
    def _():
        dt = jnp.exp(ldt_ref[...])
        lre = lre_ref[...]
        lim = lim_ref[...]
        mag = jnp.exp(lre * dt)
        are = mag * jnp.cos(lim * dt)
        aim = mag * jnp.sin(lim * dt)
        den = lre * lre + lim * lim
        nr = are - 1.0
        cre = (nr * lre + aim * lim) / den
        cim = (aim * lre - nr * lim) / den
        ab_ref[0:1, :] = are
        ab_ref[1:2, :] = aim
        for s in range(S5_SLABS):
            sl = slice(s * S5_SLAB_N, (s + 1) * S5_SLAB_N)
            bre = bre_ref[s]
            bim = bim_ref[s]
            bbar_ref[s, :, 0:S5_SLAB_N] = (bre * cre[:, sl] - bim * cim[:, sl]).astype(BF16)
            bbar_ref[s, :, S5_SLAB_N:2 * S5_SLAB_N] = (bre * cim[:, sl] + bim * cre[:, sl]).astype(BF16)
        st_ref[...] = x0_ref[...]

    for s in range(S5_SLABS):
        bu = _dot(u_ref[:, s * LANES:(s + 1) * LANES].astype(BF16), bbar_ref[s])
        bu_ref[:, s * S5_SLAB_N:(s + 1) * S5_SLAB_N] = bu[:, 0:S5_SLAB_N]
        bu_ref[:, S5_N + s * S5_SLAB_N:S5_N + (s + 1) * S5_SLAB_N] = bu[:, S5_SLAB_N:2 * S5_SLAB_N]
    are = ab_ref[0:1, :]
    aim = ab_ref[1:2, :]

    def token(t, carry):
        rows = pl.ds(pl.multiple_of(t * nrow, nrow), nrow)
        xr = st_ref[:, 0:S5_N]
        xi = st_ref[:, S5_N:2 * S5_N]
        nr = are * xr - aim * xi + bu_ref[rows, 0:S5_N]
        ni = are * xi + aim * xr + bu_ref[rows, S5_N:2 * S5_N]
        st_ref[:, 0:S5_N] = nr
        st_ref[:, S5_N:2 * S5_N] = ni
        xs_ref[rows, 0:S5_N] = nr
        xs_ref[rows, S5_N:2 * S5_N] = ni
        return carry

    lax.fori_loop(0, tc, token, 0)

    ys = []
    for s in range(S5_SLABS):
        sl = slice(s * S5_SLAB_N, (s + 1) * S5_SLAB_N)
        sli = slice(S5_N + s * S5_SLAB_N, S5_N + (s + 1) * S5_SLAB_N)
        ys.append(_dot(xs_ref[:, sl].astype(BF16), cre_ref[s]) - _dot(xs_ref[:, sli].astype(BF16), cim_ref[s]))
    y = jnp.concatenate(ys, axis=1) + d_ref[...] * u_ref[...]
    y = 0.5 * y * (1.0 + jnp.tanh(math.sqrt(2.0 / math.pi) * (y + 0.044715 * (y * y * y))))
    gate = _dot(y.astype(BF16), gw_ref[...]) + gbias_ref[...]
    o_ref[...] = (y * _sigmoid(gate)).astype(o_ref.dtype)

    @pl.when(j == pl.num_programs(0) - 1)
    def _():
        xf_ref[...] = st_ref[...]


def _s5(u, x0, lre, lim, ldt, bre, bim, cre, cim, d, gw, gbias, nrow, t):
    tc = _pick_block(t, 1, 96)
    rows = tc * nrow
    vec = _const_spec((1, S5_N))
    o0 = jnp.zeros((u.shape[0], S5_WIDTH), BF16)
    return pl.pallas_call(
        functools.partial(_s5_body, nrow=nrow, tc=tc),
        grid=(t // tc,),
        in_specs=[_row_spec(rows, S5_WIDTH), _const_spec((nrow, 2 * S5_N)), vec, vec, vec,
                  _const_spec(bre.shape), _const_spec(bim.shape), _const_spec(cre.shape), _const_spec(cim.shape),
                  _const_spec((1, S5_WIDTH)), _const_spec(gw.shape), _const_spec((1, S5_WIDTH)),
                  pl.BlockSpec(memory_space=pl.ANY)],
        out_specs=[_row_spec(rows, S5_WIDTH), pl.BlockSpec((nrow, 2 * S5_N), lambda i: (0, 0))],
        out_shape=[jax.ShapeDtypeStruct((u.shape[0], S5_WIDTH), BF16), jax.ShapeDtypeStruct((nrow, 2 * S5_N), F32)],
        scratch_shapes=[pltpu.VMEM((nrow, 2 * S5_N), F32), pltpu.VMEM((SUBLANES, S5_N), F32),
                        pltpu.VMEM((S5_SLABS, LANES, 2 * S5_SLAB_N), BF16), pltpu.VMEM((rows, 2 * S5_N), F32),
                        pltpu.VMEM((rows, 2 * S5_N), F32)],
        input_output_aliases={12: 0},
        compiler_params=_params(("arbitrary",)),
        name="s5",
    )(u, x0, lre, lim, ldt, bre, bim, cre, cim, d, gw, gbias, o0)


def _merge_body(x_ref, og_ref, or_ref, os_ref, gn_ref, wt_ref, wg_ref, wr_ref, ws_ref, wo_ref, o_ref, *, d):
    x = x_ref[...]
    h = _rms(x, gn_ref[...]).astype(BF16)
    m = None
    for i, (b_ref, w_ref) in enumerate(((og_ref, wg_ref), (or_ref, wr_ref), (os_ref, ws_ref))):
        gate = _sigmoid(_dot(h, wt_ref[:, i * d:(i + 1) * d]))
        term = gate * _dot(b_ref[...], w_ref[...])
        m = term if m is None else m + term
    o_ref[...] = x + _dot(m.astype(BF16), wo_ref[...])


def _merge(x, og, orw, os5, gn, wt, wg, wr, ws, wo):
    m, d = x.shape
    tm = _pick_block(m, 16, 512)
    return pl.pallas_call(
        functools.partial(_merge_body, d=d),
        grid=(m // tm,),
        in_specs=[_row_spec(tm, d), _row_spec(tm, GLA_V), _row_spec(tm, RWKV_W), _row_spec(tm, S5_WIDTH),
                  _const_spec((1, d))] + [_const_spec(w.shape) for w in (wt, wg, wr, ws, wo)],
        out_specs=_row_spec(tm, d),
        out_shape=jax.ShapeDtypeStruct((m, d), F32),
        compiler_params=_params(("parallel",)),
        name="merge",
    )(x, og, orw, os5, gn, wt, wg, wr, ws, wo)


def _ffn_body(x_ref, g_ref, w1_ref, w3_ref, w2_ref, o_ref, *, dff, cf):
    x = x_ref[...]
    h = _rms(x, g_ref[...]).astype(BF16)
    acc = x
    for c0 in range(0, dff, cf):
        a = _dot(h, w1_ref[:, c0:c0 + cf])
        b = _dot(h, w3_ref[:, c0:c0 + cf])
        acc = acc + _dot((a * _sigmoid(a) * b).astype(BF16), w2_ref[c0:c0 + cf, :])
    o_ref[...] = acc


def _ffn(x, g, w1, w3, w2):
    m, d = x.shape
    tm = _pick_block(m, SUBLANES, 1024)
    dff = w1.shape[1]
    cf = _pick_block(dff, LANES, 512)
    return pl.pallas_call(
        functools.partial(_ffn_body, dff=dff, cf=cf),
        grid=(m // tm,),
        in_specs=[_row_spec(tm, d), _const_spec((1, d)), _const_spec(w1.shape), _const_spec(w3.shape),
                  _const_spec(w2.shape)],
        out_specs=_row_spec(tm, d),
        out_shape=jax.ShapeDtypeStruct((m, d), F32),
        compiler_params=_params(("parallel",)),
        name="ffn",
    )(x, g, w1, w3, w2)


def _final_norm_body(x_ref, g_ref, o_ref):
    o_ref[...] = _rms(x_ref[...], g_ref[...])


def _final_norm(x, g):
    m, d = x.shape
    tm = _pick_block(m, SUBLANES, 1024)
    return pl.pallas_call(
        _final_norm_body,
        grid=(m // tm,),
        in_specs=[_row_spec(tm, d), _const_spec((1, d))],
        out_specs=_row_spec(tm, d),
        out_shape=jax.ShapeDtypeStruct((m, d), F32),
        compiler_params=_params(("parallel",)),
        name="final_norm",
    )(x, g)


def _block_ones(n, blk, dtype):
    i = jnp.arange(n) // blk
    return (i[:, None] == i[None, :]).astype(dtype)


def _gla_consts():
    c = GLA_SUB
    i = jnp.arange(GLA_TILE)
    same = i[:, None] // c == i[None, :] // c
    tri = ((i[:, None] >= i[None, :]) & same).astype(F32)
    tot = same.astype(F32)
    first = (jnp.arange(GLA_TILE // c)[:, None] * c == i[None, :]).astype(F32)
    segb = (jnp.arange(GLA_QK)[:, None] // GLA_DK == jnp.arange(GLA_V)[None, :] // GLA_DV).astype(BF16)
    sel = (jnp.arange(16)[:, None] // 3 == jnp.arange(3 * GLA_DV)[None, :] // GLA_DV).astype(BF16)
    return (tri, tot, first, segb), sel


def _block_diag_slabs(p, nslab):
    g, a, b = p.shape
    gs = g // nslab
    eye = jnp.eye(gs, dtype=p.dtype)
    p = p.reshape(nslab, gs, a, b)
    return (p[:, :, :, None, :] * eye[None, :, None, :, None]).reshape(nslab, gs * a, gs * b)


def kernel(x_prompt, x_sample, state_gla, state_rwkv, state_rwkv_shift, state_s5_re, state_s5_im, meta_tokens, norm_mix, norm_ffn, w_in, gla_gate_w2, gla_gate_b, gla_norm, rwkv_mu, rwkv_w0, rwkv_w2, rwkv_a0, rwkv_a2, rwkv_g2, rwkv_k_k, rwkv_k_a, rwkv_r_k, rwkv_ln_w, rwkv_ln_b, s5_a_re, s5_a_im, s5_log_dt, s5_b_re, s5_b_im, s5_c_re, s5_c_im, s5_d, s5_glu_w, s5_glu_b, w_br_gla, w_br_rwkv, w_br_s5, w_out, ffn_w1, ffn_w3, ffn_w2, final_norm):
    nb, seq, d = x_prompt.shape
    ns = x_sample.shape[0]
    assert x_sample.shape[1] == 1
    depth = w_in.shape[0]
    t = seq + N_META
    tp = -(-t // LANES) * LANES
    mp = nb * tp
    assert nb * RWKV_HEADS == CHAINS and (ns * RWKV_HEADS) % CHAINS == 0

    meta = jnp.broadcast_to(meta_tokens.astype(x_prompt.dtype)[None], (nb, N_META, d))
    xp = jnp.concatenate([meta, x_prompt, jnp.zeros((nb, tp - t, d), x_prompt.dtype)], axis=1).reshape(mp, d)
    xs = x_sample.reshape(ns, d)

    gla_unroll = 3 if t % (3 * GLA_SUB) == 0 else 1
    gla_consts, sel = _gla_consts()
    bd = _block_ones(RWKV_W, RWKV_N, BF16)
    o_q, o_z = 0, 2 * GLA_QK + 2 * GLA_V
    o_r = o_z + GLA_GATE_RANK
    o_s = o_r + RWKV_COLS
    o_t = o_s + S5_WIDTH

    new_p = [[] for _ in range(5)]
    new_s = [[] for _ in range(5)]
    for l in range(depth):
        wl = w_in[l]
        wg = jnp.concatenate([wl[:, o_q:o_r], jnp.zeros((d, GLA_ZPAD - GLA_GATE_RANK), F32)], axis=1).astype(BF16)
        w_groups = (wg, wl[:, o_r:o_s].astype(BF16), wl[:, o_s:o_t].astype(BF16))
        w_gates = wl[:, o_t:].astype(BF16)
        pg_p, pr_p, ps_p = _inproj(xp, norm_mix[l][None], w_groups)
        pg_s, pr_s, ps_s = _inproj(xs, norm_mix[l][None], w_groups)

        w2p = jnp.concatenate([gla_gate_w2[l], jnp.zeros((GLA_ZPAD - GLA_GATE_RANK, GLA_QK), F32)], axis=0).astype(BF16)
        gb = gla_gate_b[l][None]
        gn = gla_norm[l][None]
        og_p, gla_p = _gla_state(*_gla_intra(pg_p, w2p, gb, gn, gla_consts), nb, t, tp, gla_unroll)
        gla_p = jnp.swapaxes(gla_p[:, :, :, :GLA_DK], 2, 3)
        og_s, gla_s = _gla_sample(pg_s, state_gla[l].reshape(ns, GLA_QK, GLA_DV), w2p, gb, gn, sel)
        new_p[0].append(gla_p.reshape(nb, GLA_HEADS, GLA_DK, GLA_DV))
        new_s[0].append(gla_s.reshape(ns, GLA_HEADS, GLA_DK, GLA_DV))

        wlow = jnp.zeros((RWKV_LOW, 3 * RWKV_W), F32)
        wlow = wlow.at[0:RWKV_W_RANK, 0:RWKV_W].set(rwkv_w2[l])
        wlow = wlow.at[RWKV_W_RANK:RWKV_W_RANK + RWKV_A_RANK, RWKV_W:2 * RWKV_W].set(rwkv_a2[l])
        wlow = wlow.at[RWKV_W_RANK + RWKV_A_RANK:, 2 * RWKV_W:].set(rwkv_g2[l])
        prep_consts = (rwkv_mu[l][None], rwkv_w0[l][None], rwkv_a0[l][None], rwkv_k_k[l][None], rwkv_k_a[l][None],
                       wlow.astype(BF16), bd)
        post_consts = (rwkv_r_k[l].reshape(1, RWKV_W), rwkv_ln_w[l][None], rwkv_ln_b[l][None], bd)
        (r_, k_, v_, g_), tr = _rwkv_prep_seq(pr_p, prep_consts, nb, tp)
        chain = _to_chain(tr[:3], nb, tp) + _to_chain(tr[3:], nb, tp)
        yt, sf = _rwkv_scan(*chain, _state_to_chains(jnp.zeros((nb, RWKV_HEADS, RWKV_N, RWKV_N), F32), nb), t)
        orw_p = _rwkv_post(_from_chain(yt.reshape(tp * HALF_N, LANES), nb, tp), r_, k_, v_, g_, *post_consts)
        new_p[1].append(_state_from_chains(sf, nb))
        new_p[2].append(pr_p.reshape(nb, tp, RWKV_COLS)[:, t - 1])
        r_, w_, k_, v_, kk_, kka_, g_ = _rwkv_prep_step(pr_s, state_rwkv_shift[l], prep_consts)
        chain = [_sample_to_chain(a, ns) for a in (w_, kk_, kka_, k_, r_, v_)]
        yt, sf = _rwkv_scan(*chain, _state_to_chains(state_rwkv[l], ns), 1)
        orw_s = _rwkv_post(_sample_from_chain(yt, ns), r_, k_, v_, g_, *post_consts)
        new_s[1].append(_state_from_chains(sf, ns))
        new_s[2].append(pr_s)

        lre = s5_a_re[l].reshape(1, S5_N)
        lim = s5_a_im[l].reshape(1, S5_N)
        ldt = jnp.broadcast_to(s5_log_dt[l][:, None], (S5_GROUPS, S5_STATE)).reshape(1, S5_N)
        bre = _block_diag_slabs(jnp.swapaxes(s5_b_re[l], 1, 2), S5_SLABS)
        bim = _block_diag_slabs(jnp.swapaxes(s5_b_im[l], 1, 2), S5_SLABS)
        cre = _block_diag_slabs(jnp.swapaxes(s5_c_re[l], 1, 2), S5_SLABS).astype(BF16)
        cim = _block_diag_slabs(jnp.swapaxes(s5_c_im[l], 1, 2), S5_SLABS).astype(BF16)
        s5_args = (lre, lim, ldt, bre, bim, cre, cim, s5_d[l][None], s5_glu_w[l].astype(BF16), s5_glu_b[l][None])
        u_p = ps_p.reshape(nb, tp, S5_WIDTH).transpose(1, 0, 2).reshape(mp, S5_WIDTH)
        os_p, xf_p = _s5(u_p, jnp.zeros((nb, 2 * S5_N), F32), *s5_args, nb, t)
        os_p = os_p.reshape(tp, nb, S5_WIDTH).transpose(1, 0, 2).reshape(mp, S5_WIDTH)
        x0_s = jnp.concatenate([state_s5_re[l].reshape(ns, S5_N), state_s5_im[l].reshape(ns, S5_N)], axis=1)
        os_s, xf_s = _s5(ps_s, x0_s, *s5_args, ns, 1)
        new_p[3].append(xf_p[:, :S5_N].reshape(nb, S5_GROUPS, S5_STATE))
        new_p[4].append(xf_p[:, S5_N:].reshape(nb, S5_GROUPS, S5_STATE))
        new_s[3].append(xf_s[:, :S5_N].reshape(ns, S5_GROUPS, S5_STATE))
        new_s[4].append(xf_s[:, S5_N:].reshape(ns, S5_GROUPS, S5_STATE))

        w_merge = (w_br_gla[l].astype(BF16), w_br_rwkv[l].astype(BF16), w_br_s5[l].astype(BF16), w_out[l].astype(BF16))
        w_ffn = (norm_ffn[l][None], ffn_w1[l].astype(BF16), ffn_w3[l].astype(BF16), ffn_w2[l].astype(BF16))
        xp = _ffn(_merge(xp, og_p, orw_p, os_p, norm_mix[l][None], w_gates, *w_merge), *w_ffn)
        xs = _ffn(_merge(xs, og_s, orw_s, os_s, norm_mix[l][None], w_gates, *w_merge), *w_ffn)

    y_prompt = _final_norm(xp, final_norm[None]).reshape(nb, tp, d)[:, N_META:t]
    y_sample = _final_norm(xs, final_norm[None]).reshape(ns, 1, d)
    outs_p = [jnp.stack(a, axis=0) for a in new_p]
    outs_s = [jnp.stack(a, axis=0) for a in new_s]
    return (y_prompt, y_sample, *outs_p, *outs_s)
```

```python
import functools
import math

import jax
import jax.numpy as jnp
from jax import lax
from jax.experimental import pallas as pl
from jax.experimental.pallas import tpu as pltpu

F32 = jnp.float32
BF16 = jnp.bfloat16
HIGHEST = lax.Precision.HIGHEST

N_META = 16
NORM_EPS = 1e-6
GLA_HEADS = 4
GLA_DK = 64
GLA_DV = 128
GLA_QK = GLA_HEADS * GLA_DK
GLA_V = GLA_HEADS * GLA_DV
GLA_GATE_RANK = 16
GLA_TAU = 16.0
GLA_SUB = 16
RWKV_HEADS = 8
RWKV_N = 64
RWKV_W = RWKV_HEADS * RWKV_N
RWKV_W_RANK = 64
RWKV_A_RANK = 64
RWKV_G_RANK = 128
RWKV_LOW = RWKV_W_RANK + RWKV_A_RANK + RWKV_G_RANK
RWKV_COLS = 3 * RWKV_W + RWKV_LOW
RWKV_GN_EPS = 64e-5
S5_GROUP = 16
S5_GROUPS = 32
S5_WIDTH = S5_GROUPS * S5_GROUP
S5_STATE = 64
S5_N = S5_GROUPS * S5_STATE

LANES = 128
SUBLANES = 8
CHAINS = LANES // 2
HALF_N = RWKV_N // 2
VTILES = HALF_N // SUBLANES
S5_SLABS = S5_WIDTH // LANES
S5_SLAB_N = S5_N // S5_SLABS
GLA_ZPAD = LANES
GLA_COLS = 2 * GLA_QK + 2 * GLA_V + GLA_ZPAD
VMEM_LIMIT = 56 << 20


def _params(sem):
    return pltpu.CompilerParams(dimension_semantics=sem, vmem_limit_bytes=VMEM_LIMIT)


def _const_spec(shape):
    nd = len(shape)
    return pl.BlockSpec(shape, lambda *_: (0,) * nd, pipeline_mode=pl.Buffered(1))


def _row_spec(tm, cols):
    return pl.BlockSpec((tm, cols), lambda i: (i, 0))


def _pick_block(n, mult, cap):
    best = None
    for d in range(mult, min(n, cap) + 1, mult):
        if n % d == 0:
            best = d
    assert best is not None, (n, mult, cap)
    return best


def _sigmoid(x):
    return 1.0 / (1.0 + jnp.exp(-x))


def _softplus(x):
    return jnp.maximum(x, 0.0) + jnp.log1p(jnp.exp(-jnp.abs(x)))


def _rms(x, g):
    return x * lax.rsqrt(jnp.mean(x * x, axis=-1, keepdims=True) + NORM_EPS) * g


def _dot(a, b):
    return jnp.dot(a, b, preferred_element_type=F32)


def _dot_t0(a, b):
    return lax.dot_general(a, b, (((0,), (0,)), ((), ())), preferred_element_type=F32)


def _split3_rows(x):
    hi = x.astype(BF16).astype(F32)
    r1 = x - hi
    mid = r1.astype(BF16).astype(F32)
    lo = (r1 - mid).astype(BF16).astype(F32)
    return [hi, mid, lo]


def _inproj_body(x_ref, g_ref, *refs):
    n = len(refs) // 2
    h = _rms(x_ref[...], g_ref[...]).astype(BF16)
    for w_ref, o_ref in zip(refs[:n], refs[n:]):
        o_ref[...] = _dot(h, w_ref[...])


def _inproj(x, g, ws):
    m, d = x.shape
    tm = _pick_block(m, SUBLANES, 512)
    widths = [w.shape[1] for w in ws]
    return pl.pallas_call(
        _inproj_body,
        grid=(m // tm,),
        in_specs=[_row_spec(tm, d), _const_spec((1, d))] + [_const_spec(w.shape) for w in ws],
        out_specs=[_row_spec(tm, c) for c in widths],
        out_shape=[jax.ShapeDtypeStruct((m, c), F32) for c in widths],
        compiler_params=_params(("parallel",)),
        name="inproj",
    )(x, g, *ws)


def _log_sigmoid(x):
    return jnp.minimum(x, 0.0) - jnp.log1p(jnp.exp(-jnp.abs(x)))


def _gla_finish(o, g, gn):
    parts = []
    for h in range(GLA_HEADS):
        oh = o[:, h * GLA_DV:(h + 1) * GLA_DV]
        parts.append(oh * lax.rsqrt(jnp.mean(oh * oh, axis=-1, keepdims=True) + NORM_EPS))
    on = jnp.concatenate(parts, axis=1)
    return on * gn * (g * _sigmoid(g))


def _gla_log_decay(z, w2_ref, gb_ref):
    zz = _dot(z.astype(BF16), w2_ref[...]) + gb_ref[...]
    return _log_sigmoid(zz) * (1.0 / GLA_TAU)


GLA_TILE = LANES


def _pad_heads(x):
    zero = jnp.zeros((x.shape[0], LANES - GLA_DK), x.dtype)
    parts = []
    for h in range(GLA_HEADS):
        parts += [x[:, h * GLA_DK:(h + 1) * GLA_DK], zero]
    return jnp.concatenate(parts, axis=1)


def _gla_intra_body(p_ref, w2_ref, gb_ref, gn_ref, tri_ref, tot_ref, first_ref, segb_ref,
                    oi_ref, qe_ref, kd_ref, vb_ref, gate_ref, ebl_ref):
    c = GLA_SUB
    q = p_ref[:, 0:GLA_QK]
    k = p_ref[:, GLA_QK:2 * GLA_QK]
    v = p_ref[:, 2 * GLA_QK:2 * GLA_QK + GLA_V]
    g = p_ref[:, 2 * GLA_QK + GLA_V:2 * GLA_QK + 2 * GLA_V]
    z = p_ref[:, 2 * GLA_QK + 2 * GLA_V:GLA_COLS]
    la = _gla_log_decay(z, w2_ref, gb_ref)
    bc = jnp.dot(tri_ref[...], la, precision=HIGHEST, preferred_element_type=F32)
    bl = jnp.dot(tot_ref[...], la, precision=HIGHEST, preferred_element_type=F32)
    qs = q * (GLA_DK ** -0.5)
    qe_ref[...] = _pad_heads(qs * jnp.exp(bc)).astype(BF16)
    kd_ref[...] = _pad_heads(k * jnp.exp(bl - bc)).astype(BF16)
    ebl_ref[...] = jnp.dot(first_ref[...], _pad_heads(jnp.exp(bl)), precision=HIGHEST, preferred_element_type=F32)
    vb_ref[...] = v.astype(BF16)
    gate_ref[...] = gn_ref[...] * (g * _sigmoid(g))
    row_id = lax.broadcasted_iota(jnp.int32, (c, GLA_QK), 0)
    nchunk = p_ref.shape[0] // c
    per_dot = 4
    for c0 in range(0, nchunk, per_dot):
        terms = []
        for u in range(c0, c0 + per_dot):
            sl = slice(u * c, (u + 1) * c)
            for jj in range(c):
                e = jnp.exp(bc[sl] - bc[u * c + jj:u * c + jj + 1, :])
                terms.append(jnp.where(row_id >= jj, qs[sl] * k[u * c + jj:u * c + jj + 1, :] * e, 0.0))
        pw = _dot(jnp.concatenate(terms, axis=0).astype(BF16), segb_ref[...])
        for i, u in enumerate(range(c0, c0 + per_dot)):
            base = i * c * c
            acc = pw[base:base + c] * v[u * c:u * c + 1, :]
            for jj in range(1, c):
                acc = acc + pw[base + jj * c:base + (jj + 1) * c] * v[u * c + jj:u * c + jj + 1, :]
            oi_ref[u * c:(u + 1) * c, :] = acc


def _gla_intra(pg, w2p, gb, gn, consts):
    m = pg.shape[0]
    tm = GLA_TILE
    assert m % tm == 0
    tri, tot, first, segb = consts
    nck = tm // GLA_SUB
    wide = GLA_HEADS * LANES
    return pl.pallas_call(
        _gla_intra_body,
        grid=(m // tm,),
        in_specs=[_row_spec(tm, GLA_COLS), _const_spec(w2p.shape), _const_spec(gb.shape), _const_spec(gn.shape),
                  _const_spec(tri.shape), _const_spec(tot.shape), _const_spec(first.shape), _const_spec(segb.shape)],
        out_specs=[_row_spec(tm, GLA_V), _row_spec(tm, wide), _row_spec(tm, wide), _row_spec(tm, GLA_V),
                   _row_spec(tm, GLA_V), _row_spec(nck, wide)],
        out_shape=[jax.ShapeDtypeStruct((m, GLA_V), F32), jax.ShapeDtypeStruct((m, wide), BF16),
                   jax.ShapeDtypeStruct((m, wide), BF16), jax.ShapeDtypeStruct((m, GLA_V), BF16),
                   jax.ShapeDtypeStruct((m, GLA_V), F32), jax.ShapeDtypeStruct((m // GLA_SUB, wide), F32)],
        compiler_params=_params(("parallel",)),
        name="gla_intra",
    )(pg, w2p, gb, gn, tri, tot, first, segb)


def _gla_state_body(oi_ref, qe_ref, kd_ref, vb_ref, gate_ref, ebl_ref, o_ref, sf_ref, st_ref, *, ngroup, unroll):
    c = GLA_SUB
    gr = c * unroll

    st_ref[...] = jnp.zeros_like(st_ref)

    def group(gi, carry):
        rows = pl.ds(pl.multiple_of(gi * gr, gr), gr)
        qe = qe_ref[rows, :]
        kd = kd_ref[rows, :]
        vb = vb_ref[rows, :]
        st = [st_ref[h] for h in range(GLA_HEADS)]
        outs = []
        for u in range(unroll):
            sl = slice(u * c, (u + 1) * c)
            dec = ebl_ref[gi * unroll + u]
            parts = []
            for h in range(GLA_HEADS):
                hs = slice(h * LANES, (h + 1) * LANES)
                parts.append(lax.dot_general(qe[sl, hs], st[h].astype(BF16), (((1,), (1,)), ((), ())),
                                             preferred_element_type=F32))
                st[h] = st[h] * dec[:, hs] + _dot_t0(vb[sl, hs], kd[sl, hs])
            outs.append(jnp.concatenate(parts, axis=1))
        o = (outs[0] if unroll == 1 else jnp.concatenate(outs, axis=0)) + oi_ref[rows, :]
        normed = []
        for h in range(GLA_HEADS):
            oh = o[:, h * GLA_DV:(h + 1) * GLA_DV]
            normed.append(oh * lax.rsqrt(jnp.mean(oh * oh, axis=-1, keepdims=True) + NORM_EPS))
        o_ref[rows, :] = (jnp.concatenate(normed, axis=1) * gate_ref[rows, :]).astype(o_ref.dtype)
        for h in range(GLA_HEADS):
            st_ref[h] = st[h]
        return carry

    lax.fori_loop(0, ngroup, group, 0)
    t = ngroup * gr
    if t < o_ref.shape[0]:
        o_ref[t:, :] = jnp.zeros((o_ref.shape[0] - t, GLA_V), o_ref.dtype)
    sf_ref[0] = st_ref[...]


def _gla_state(oi, qe, kd, vb, gate, ebl, nb, t, tp, unroll):
    gr = GLA_SUB * unroll
    assert t % gr == 0
    wide = GLA_HEADS * LANES
    nck = tp // GLA_SUB
    seq = lambda cols: pl.BlockSpec((tp, cols), lambda b: (b, 0))
    return pl.pallas_call(
        functools.partial(_gla_state_body, ngroup=t // gr, unroll=unroll),
        grid=(nb,),
        in_specs=[seq(GLA_V), seq(wide), seq(wide), seq(GLA_V), seq(GLA_V),
                  pl.BlockSpec((nck, 1, wide), lambda b: (b, 0, 0))],
        out_specs=[seq(GLA_V), pl.BlockSpec((1, GLA_HEADS, GLA_DV, LANES), lambda b: (b, 0, 0, 0))],
        out_shape=[jax.ShapeDtypeStruct((nb * tp, GLA_V), BF16),
                   jax.ShapeDtypeStruct((nb, GLA_HEADS, GLA_DV, LANES), F32)],
        scratch_shapes=[pltpu.VMEM((GLA_HEADS, GLA_DV, LANES), F32)],
        compiler_params=_params(("parallel",)),
        name="gla_state",
    )(oi, qe, kd, vb, gate, ebl.reshape(nb * nck, 1, wide))


def _gla_sample_body(p_ref, s0_ref, w2_ref, gb_ref, gn_ref, sel_ref, o_ref, s_ref, *, nrow):
    q = p_ref[:, 0:GLA_QK]
    k = p_ref[:, GLA_QK:2 * GLA_QK]
    v = p_ref[:, 2 * GLA_QK:2 * GLA_QK + GLA_V]
    g = p_ref[:, 2 * GLA_QK + GLA_V:2 * GLA_QK + 2 * GLA_V]
    z = p_ref[:, 2 * GLA_QK + 2 * GLA_V:GLA_COLS]
    a = jnp.exp(_gla_log_decay(z, w2_ref, gb_ref))
    qs = q * (GLA_DK ** -0.5)
    out_rows = []
    for i in range(nrow):
        pieces = _split3_rows(a[i:i + 1]) + _split3_rows(k[i:i + 1]) + _split3_rows(qs[i:i + 1])
        ef = jnp.concatenate(pieces + [jnp.zeros((16 - 9, GLA_QK), F32)], axis=0)
        cb = _dot_t0(ef.astype(BF16), sel_ref[...])
        ab = cb[:, 0:GLA_DV]
        kb = cb[:, GLA_DV:2 * GLA_DV]
        qb = cb[:, 2 * GLA_DV:3 * GLA_DV]
        vb = jnp.concatenate(
            [jnp.broadcast_to(v[i:i + 1, h * GLA_DV:(h + 1) * GLA_DV], (GLA_DK, GLA_DV)) for h in range(GLA_HEADS)],
            axis=0)
        s = ab * s0_ref[i] + kb * vb
        s_ref[i] = s
        op = qb * s
        out_rows.append(jnp.concatenate(
            [jnp.sum(op[h * GLA_DK:(h + 1) * GLA_DK], axis=0, keepdims=True) for h in range(GLA_HEADS)], axis=1))
    o = jnp.concatenate(out_rows, axis=0)
    o_ref[...] = _gla_finish(o, g, gn_ref[...]).astype(o_ref.dtype)


def _gla_sample(pg, s0, w2p, gb, gn, sel):
    ns = pg.shape[0]
    nrow = 16
    assert ns % nrow == 0
    return pl.pallas_call(
        functools.partial(_gla_sample_body, nrow=nrow),
        grid=(ns // nrow,),
        in_specs=[_row_spec(nrow, GLA_COLS),
                  pl.BlockSpec((nrow, GLA_QK, GLA_DV), lambda i: (i, 0, 0)),
                  _const_spec(w2p.shape), _const_spec(gb.shape), _const_spec(gn.shape), _const_spec(sel.shape)],
        out_specs=[_row_spec(nrow, GLA_V), pl.BlockSpec((nrow, GLA_QK, GLA_DV), lambda i: (i, 0, 0))],
        out_shape=[jax.ShapeDtypeStruct((ns, GLA_V), BF16), jax.ShapeDtypeStruct((ns, GLA_QK, GLA_DV), F32)],
        compiler_params=_params(("parallel",)),
        name="gla_sample",
    )(pg, s0, w2p, gb, gn, sel)


def _segsum(x, bd):
    hi = x.astype(BF16)
    lo = (x - hi.astype(F32)).astype(BF16)
    return _dot(hi, bd) + _dot(lo, bd)


N_PREP_CONSTS = 8


def _rwkv_prep_math(p, prev, mu_ref, w0_ref, a0_ref, kkw_ref, kaw_ref, rk_ref, wlow_ref, bd_ref):
    xs = p + (prev - p) * mu_ref[...]
    r = xs[:, 0:RWKV_W]
    k = xs[:, RWKV_W:2 * RWKV_W]
    v = xs[:, 2 * RWKV_W:3 * RWKV_W]
    zl = xs[:, 3 * RWKV_W:RWKV_COLS]
    lane = lax.broadcasted_iota(jnp.int32, zl.shape, 1)
    zt = jnp.where(lane < RWKV_W_RANK, jnp.tanh(zl),
                   jnp.where(lane < RWKV_W_RANK + RWKV_A_RANK, zl, _sigmoid(zl)))
    low = _dot(zt.astype(BF16), wlow_ref[...])
    w_log = -_softplus(-(w0_ref[...] + low[:, 0:RWKV_W])) - 0.5
    a = _sigmoid(a0_ref[...] + low[:, RWKV_W:2 * RWKV_W])
    kk = k * kkw_ref[...]
    kk = kk * lax.rsqrt(_segsum(kk * kk, bd_ref[...]) + 1e-12)
    k = k * (1.0 + (a - 1.0) * kaw_ref[...])
    bonus = _segsum(r * k * rk_ref[...], bd_ref[...]) * v
    return (jnp.exp(-jnp.exp(w_log)), kk, kk * a, k, r, v), (bonus, low[:, 2 * RWKV_W:3 * RWKV_W])


PREP_SEQS = 4


def _rwkv_prep_seq_body(p_ref, *rest):
    consts, outs, carry_ref = rest[:N_PREP_CONSTS], rest[N_PREP_CONSTS:-1], rest[-1]
    nat, tr = outs[:2], outs[2:]

    @pl.when(pl.program_id(1) == 0)
    def _():
        carry_ref[...] = jnp.zeros_like(carry_ref)

    row = lax.broadcasted_iota(jnp.int32, p_ref.shape[1:], 0)
    for b in range(p_ref.shape[0]):
        p = p_ref[b]
        prev = jnp.where(row == 0, carry_ref[b:b + 1, :], pltpu.roll(p, 1, axis=0))
        carry_ref[b:b + 1, :] = p[p.shape[0] - 1:, :]
        scan_ops, out_ops = _rwkv_prep_math(p, prev, *consts)
        for ref, val in zip(nat, out_ops):
            ref[b] = val
        for ref, val in zip(tr, scan_ops):
            ref[b] = val.T


def _rwkv_prep_step_body(p_ref, pp_ref, *rest):
    scan_ops, out_ops = _rwkv_prep_math(p_ref[...], pp_ref[...], *rest[:N_PREP_CONSTS])
    for ref, val in zip(rest[N_PREP_CONSTS:], scan_ops + out_ops):
        ref[...] = val


def _rwkv_prep_consts_specs(consts):
    vec = _const_spec((1, RWKV_W))
    return [_const_spec((1, RWKV_COLS))] + [vec] * 5 + [_const_spec(consts[6].shape), _const_spec(consts[7].shape)]


def _rwkv_prep_seq(p, consts, nb, tp):
    tb = LANES
    ng = PREP_SEQS
    assert nb % ng == 0 and tp % tb == 0
    nat = pl.BlockSpec((ng, tb, RWKV_W), lambda i, j: (i, j, 0))
    tr = pl.BlockSpec((ng, RWKV_W, tb), lambda i, j: (i, 0, j))
    outs = pl.pallas_call(
        _rwkv_prep_seq_body,
        grid=(nb // ng, tp // tb),
        in_specs=[pl.BlockSpec((ng, tb, RWKV_COLS), lambda i, j: (i, j, 0))] + _rwkv_prep_consts_specs(consts),
        out_specs=[nat] * 2 + [tr] * 6,
        out_shape=[jax.ShapeDtypeStruct((nb, tp, RWKV_W), F32)] * 2 + [jax.ShapeDtypeStruct((nb, RWKV_W, tp), F32)] * 6,
        scratch_shapes=[pltpu.VMEM((ng, RWKV_COLS), F32)],
        compiler_params=_params(("parallel", "arbitrary")),
        name="rwkv_prep_seq",
    )(p.reshape(nb, tp, RWKV_COLS), *consts)
    return [o.reshape(nb * tp, RWKV_W) for o in outs[:2]], list(outs[2:])


def _rwkv_prep_step(p, prev, consts):
    nb = p.shape[0]
    tm = _pick_block(nb, SUBLANES, 256)
    return pl.pallas_call(
        _rwkv_prep_step_body,
        grid=(nb // tm,),
        in_specs=[_row_spec(tm, RWKV_COLS), _row_spec(tm, RWKV_COLS)] + _rwkv_prep_consts_specs(consts),
        out_specs=[_row_spec(tm, RWKV_W)] * 8,
        out_shape=[jax.ShapeDtypeStruct((nb, RWKV_W), F32)] * 8,
        compiler_params=_params(("parallel",)),
        name="rwkv_prep_step",
    )(p, prev, *consts)


def _to_chain_body(*refs, nb, narr):
    x_refs, o_refs = refs[:narr], refs[narr:]
    tc = x_refs[0].shape[2]

    def chains(x_ref, n):
        return [x_ref[b, pl.ds(n, RWKV_HEADS, stride=RWKV_N), :] for b in range(nb)]

    for i in range(narr):
        for r in range(HALF_N):
            both = jnp.concatenate(chains(x_refs[i], r) + chains(x_refs[i], HALF_N + r), axis=0)
            o_refs[i][pl.ds(r, tc, stride=HALF_N), :] = both.T


def _to_chain(xs, nb, tp):
    assert nb * RWKV_HEADS == CHAINS and tp % LANES == 0
    tc = LANES
    narr = len(xs)
    outs = pl.pallas_call(
        functools.partial(_to_chain_body, nb=nb, narr=narr),
        grid=(tp // tc,),
        in_specs=[pl.BlockSpec((nb, RWKV_W, tc), lambda j: (0, 0, j))] * narr,
        out_specs=[pl.BlockSpec((tc * HALF_N, LANES), lambda j: (j, 0))] * narr,
        out_shape=[jax.ShapeDtypeStruct((tp * HALF_N, LANES), F32)] * narr,
        compiler_params=_params(("parallel",)),
        name="rwkv_to_chain",
    )(*xs)
    return [o.reshape(1, tp, HALF_N, LANES) for o in outs]


def _from_chain_body(y_ref, o_ref, z_ref, *, nb):
    tc = o_ref.shape[1]
    for r in range(HALF_N):
        both = y_ref[pl.ds(r, tc, stride=HALF_N), :].T
        z_ref[pl.ds(r, CHAINS, stride=RWKV_N), :] = both[0:CHAINS]
        z_ref[pl.ds(HALF_N + r, CHAINS, stride=RWKV_N), :] = both[CHAINS:LANES]
    for b in range(nb):
        o_ref[b] = z_ref[b * RWKV_W:(b + 1) * RWKV_W, :].T


def _from_chain(y, nb, tp):
    tc = LANES
    out = pl.pallas_call(
        functools.partial(_from_chain_body, nb=nb),
        grid=(tp // tc,),
        in_specs=[pl.BlockSpec((tc * HALF_N, LANES), lambda j: (j, 0))],
        out_specs=pl.BlockSpec((nb, tc, RWKV_W), lambda j: (0, j, 0)),
        out_shape=jax.ShapeDtypeStruct((nb, tp, RWKV_W), F32),
        scratch_shapes=[pltpu.VMEM((nb * RWKV_W, tc), F32)],
        compiler_params=_params(("parallel",)),
        name="rwkv_from_chain",
    )(y)
    return out.reshape(nb * tp, RWKV_W)


def _rwkv_scan_body(w_ref, kk_ref, kka_ref, k_ref, r_ref, v_ref, s0_ref, yin_ref, y_ref, sf_ref, s_ref, u_ref, *, tc):
    del yin_ref
    j = pl.program_id(1)
    key_refs = (w_ref, kk_ref, kka_ref, k_ref, r_ref)
    low_half = lax.broadcasted_iota(jnp.int32, (HALF_N, LANES), 1) < CHAINS

    def spread(t, slot):
        for i, ref in enumerate(key_refs):
            x = ref[0, t]
            sw = pltpu.roll(x, CHAINS, axis=1)
            u_ref[slot, i, 0:HALF_N, :] = jnp.where(low_half, x, sw)
            u_ref[slot, i, HALF_N:RWKV_N, :] = jnp.where(low_half, sw, x)

    @pl.when(j == 0)
    def _():
        s_ref[...] = s0_ref[0]

    spread(0, 0)

    nsum = 4

    def total(parts):
        return (parts[0] + parts[1]) + (parts[2] + parts[3])

    def token(t, carry):
        slot = t % 2
        acc = [[None] * nsum for _ in range(VTILES)]
        for key in range(RWKV_N):
            kk = u_ref[slot, 1, key:key + 1, :]
            for i in range(VTILES):
                term = s_ref[i, key] * kk
                acc[i][key % nsum] = term if key < nsum else acc[i][key % nsum] + term
        sk = [total(a) for a in acc]
        vv = [v_ref[0, t, i * SUBLANES:(i + 1) * SUBLANES, :] for i in range(VTILES)]
        yacc = [[None] * nsum for _ in range(VTILES)]
        for key in range(RWKV_N):
            w = u_ref[slot, 0, key:key + 1, :]
            kka = u_ref[slot, 2, key:key + 1, :]
            k = u_ref[slot, 3, key:key + 1, :]
            r = u_ref[slot, 4, key:key + 1, :]
            for i in range(VTILES):
                sn = s_ref[i, key] * w - sk[i] * kka + vv[i] * k
                s_ref[i, key] = sn
                term = sn * r
                yacc[i][key % nsum] = term if key < nsum else yacc[i][key % nsum] + term
        for i in range(VTILES):
            y_ref[0, t, i * SUBLANES:(i + 1) * SUBLANES, :] = total(yacc[i])
        spread(jnp.minimum(t + 1, tc - 1), 1 - slot)
        return carry

    lax.fori_loop(0, tc, token, 0)

    @pl.when(j == pl.num_programs(1) - 1)
    def _():
        sf_ref[0] = s_ref[...]


def _rwkv_scan(w, kk, kka, k, r, v, s0, t):
    g, tp = w.shape[0], w.shape[1]
    tc = _pick_block(t, 1, 96)
    tok = pl.BlockSpec((1, tc, HALF_N, LANES), lambda i, j: (i, j, 0, 0))
    st = pl.BlockSpec((1, VTILES, RWKV_N, SUBLANES, LANES), lambda i, j: (i, 0, 0, 0, 0))
    y0 = jnp.zeros((g, tp, HALF_N, LANES), F32)
    return pl.pallas_call(
        functools.partial(_rwkv_scan_body, tc=tc),
        grid=(g, t // tc),
        in_specs=[tok] * 6 + [st, pl.BlockSpec(memory_space=pl.ANY)],
        out_specs=[tok, st],
        out_shape=[jax.ShapeDtypeStruct((g, tp, HALF_N, LANES), F32),
                   jax.ShapeDtypeStruct((g, VTILES, RWKV_N, SUBLANES, LANES), F32)],
        scratch_shapes=[pltpu.VMEM((VTILES, RWKV_N, SUBLANES, LANES), F32),
                        pltpu.VMEM((2, 5, RWKV_N, LANES), F32)],
        input_output_aliases={7: 0},
        compiler_params=_params(("parallel", "arbitrary")),
        name="rwkv_scan",
    )(w, kk, kka, k, r, v, s0, y0)


def _rwkv_post_body(y_ref, bonus_ref, g_ref, lnw_ref, lnb_ref, bd_ref, o_ref):
    y = y_ref[...]
    bd = bd_ref[...]
    inv_n = 1.0 / RWKV_N
    yc = y - _segsum(y, bd) * inv_n
    var = _segsum(yc * yc, bd) * inv_n
    yn = yc * lax.rsqrt(var + RWKV_GN_EPS) * lnw_ref[...] + lnb_ref[...]
    o_ref[...] = ((yn + bonus_ref[...]) * g_ref[...]).astype(o_ref.dtype)


def _rwkv_post(y, bonus, g, lnw, lnb, bd):
    m = y.shape[0]
    tm = _pick_block(m, 16, 1024)
    vec = _const_spec((1, RWKV_W))
    return pl.pallas_call(
        _rwkv_post_body,
        grid=(m // tm,),
        in_specs=[_row_spec(tm, RWKV_W)] * 3 + [vec, vec, _const_spec(bd.shape)],
        out_specs=_row_spec(tm, RWKV_W),
        out_shape=jax.ShapeDtypeStruct((m, RWKV_W), BF16),
        compiler_params=_params(("parallel",)),
        name="rwkv_post",
    )(y, bonus, g, lnw, lnb, bd)


def _sample_to_chain(x, nb):
    bg = CHAINS // RWKV_HEADS
    g = nb // bg
    x = x.reshape(g, bg, RWKV_HEADS, 2, HALF_N).transpose(0, 4, 3, 1, 2)
    return x.reshape(g, 1, HALF_N, LANES)


def _sample_from_chain(y, nb):
    bg = CHAINS // RWKV_HEADS
    g = nb // bg
    y = y.reshape(g, HALF_N, 2, bg, RWKV_HEADS).transpose(0, 3, 4, 2, 1)
    return y.reshape(nb, RWKV_W)


def _state_to_chains(s, nb):
    bg = CHAINS // RWKV_HEADS
    g = nb // bg
    s = s.reshape(g, bg, RWKV_HEADS, 2, VTILES, SUBLANES, RWKV_N).transpose(0, 4, 6, 5, 3, 1, 2)
    return s.reshape(g, VTILES, RWKV_N, SUBLANES, LANES)


def _state_from_chains(s, nb):
    bg = CHAINS // RWKV_HEADS
    g = nb // bg
    s = s.reshape(g, VTILES, RWKV_N, SUBLANES, 2, bg, RWKV_HEADS).transpose(0, 5, 6, 4, 1, 3, 2)
    return s.reshape(nb, RWKV_HEADS, RWKV_N, RWKV_N)


def _s5_body(u_ref, x0_ref, lre_ref, lim_ref, ldt_ref, bre_ref, bim_ref, cre_ref, cim_ref, d_ref, gw_ref, gbias_ref,
             oin_ref, o_ref, xf_ref, st_ref, ab_ref, bbar_ref, bu_ref, xs_ref, ut_ref, yt_ref, *, nrow, tc):
    del oin_ref
    j = pl.program_id(0)
    for s in range(S5_SLABS):
        lanes = slice(s * LANES, (s + 1) * LANES)
        if tc == 1:
            ut_ref[s] = u_ref[:, 0, lanes]
        else:
            for b in range(nrow):
                ut_ref[s, pl.ds(b, tc, stride=nrow), :] = u_ref[b, :, lanes]

    @pl.when(j == 0)
    def _():
        dt = jnp.exp(ldt_ref[...])
        lre = lre_ref[...]
        lim = lim_ref[...]
        mag = jnp.exp(lre * dt)
        are = mag * jnp.cos(lim * dt)
        aim = mag * jnp.sin(lim * dt)
        den = lre * lre + lim * lim
        nr = are - 1.0
        cre = (nr * lre + aim * lim) / den
        cim = (aim * lre - nr * lim) / den
        ab_ref[0:1, :] = are
        ab_ref[1:2, :] = aim
        for s in range(S5_SLABS):
            sl = slice(s * S5_SLAB_N, (s + 1) * S5_SLAB_N)
            bre = bre_ref[s]
            bim = bim_ref[s]
            bbar_ref[s, :, 0:S5_SLAB_N] = (bre * cre[:, sl] - bim * cim[:, sl]).astype(BF16)
            bbar_ref[s, :, S5_SLAB_N:2 * S5_SLAB_N] = (bre * cim[:, sl] + bim * cre[:, sl]).astype(BF16)
        st_ref[...] = x0_ref[...]

    for s in range(S5_SLABS):
        bu = _dot(ut_ref[s].astype(BF16), bbar_ref[s])
        bu_ref[:, s * S5_SLAB_N:(s + 1) * S5_SLAB_N] = bu[:, 0:S5_SLAB_N]
        bu_ref[:, S5_N + s * S5_SLAB_N:S5_N + (s + 1) * S5_SLAB_N] = bu[:, S5_SLAB_N:2 * S5_SLAB_N]
    are = ab_ref[0:1, :]
    aim = ab_ref[1:2, :]

    def token(t, carry):
        rows = pl.ds(pl.multiple_of(t * nrow, nrow), nrow)
        xr = st_ref[:, 0:S5_N]
        xi = st_ref[:, S5_N:2 * S5_N]
        nr = are * xr - aim * xi + bu_ref[rows, 0:S5_N]
        ni = are * xi + aim * xr + bu_ref[rows, S5_N:2 * S5_N]
        st_ref[:, 0:S5_N] = nr
        st_ref[:, S5_N:2 * S5_N] = ni
        xs_ref[rows, 0:S5_N] = nr
        xs_ref[rows, S5_N:2 * S5_N] = ni
        return carry

    lax.fori_loop(0, tc, token, 0)

    ys = []
    for s in range(S5_SLABS):
        sl = slice(s * S5_SLAB_N, (s + 1) * S5_SLAB_N)
        sli = slice(S5_N + s * S5_SLAB_N, S5_N + (s + 1) * S5_SLAB_N)
        ys.append(_dot(xs_ref[:, sl].astype(BF16), cre_ref[s]) - _dot(xs_ref[:, sli].astype(BF16), cim_ref[s]))
    ut = jnp.concatenate([ut_ref[s] for s in range(S5_SLABS)], axis=1)
    y = jnp.concatenate(ys, axis=1) + d_ref[...] * ut
    y = 0.5 * y * (1.0 + jnp.tanh(math.sqrt(2.0 / math.pi) * (y + 0.044715 * (y * y * y))))
    gate = _dot(y.astype(BF16), gw_ref[...]) + gbias_ref[...]
    y = y * _sigmoid(gate)
    if tc == 1:
        o_ref[:, 0, :] = y.astype(o_ref.dtype)
    else:
        for s in range(S5_SLABS):
            yt_ref[s] = y[:, s * LANES:(s + 1) * LANES]
        for b in range(nrow):
            o_ref[b] = jnp.concatenate([yt_ref[s, pl.ds(b, tc, stride=nrow), :] for s in range(S5_SLABS)],
                                       axis=1).astype(o_ref.dtype)

    @pl.when(j == pl.num_programs(0) - 1)
    def _():
        xf_ref[...] = st_ref[...]


def _s5(u, x0, lre, lim, ldt, bre, bim, cre, cim, d, gw, gbias, nrow, t):
    tp = u.shape[1]
    tc = _pick_block(t, 16, 64) if t > 1 else 1
    rows = tc * nrow
    vec = _const_spec((1, S5_N))
    o0 = jnp.zeros((nrow, tp, S5_WIDTH), BF16)
    seq = pl.BlockSpec((nrow, tc, S5_WIDTH), lambda i: (0, i, 0))
    return pl.pallas_call(
        functools.partial(_s5_body, nrow=nrow, tc=tc),
        grid=(t // tc,),
        in_specs=[seq, _const_spec((nrow, 2 * S5_N)), vec, vec, vec,
                  _const_spec(bre.shape), _const_spec(bim.shape), _const_spec(cre.shape), _const_spec(cim.shape),
                  _const_spec((1, S5_WIDTH)), _const_spec(gw.shape), _const_spec((1, S5_WIDTH)),
                  pl.BlockSpec(memory_space=pl.ANY)],
        out_specs=[seq, pl.BlockSpec((nrow, 2 * S5_N), lambda i: (0, 0))],
        out_shape=[jax.ShapeDtypeStruct((nrow, tp, S5_WIDTH), BF16), jax.ShapeDtypeStruct((nrow, 2 * S5_N), F32)],
        scratch_shapes=[pltpu.VMEM((nrow, 2 * S5_N), F32), pltpu.VMEM((SUBLANES, S5_N), F32),
                        pltpu.VMEM((S5_SLABS, LANES, 2 * S5_SLAB_N), BF16), pltpu.VMEM((rows, 2 * S5_N), F32),
                        pltpu.VMEM((rows, 2 * S5_N), F32), pltpu.VMEM((S5_SLABS, rows, LANES), F32),
                        pltpu.VMEM((S5_SLABS, rows, LANES), F32)],
        input_output_aliases={12: 0},
        compiler_params=_params(("arbitrary",)),
        name="s5",
    )(u, x0, lre, lim, ldt, bre, bim, cre, cim, d, gw, gbias, o0)


def _merge_body(x_ref, og_ref, or_ref, os_ref, gn_ref, wt_ref, wg_ref, wr_ref, ws_ref, wo_ref, o_ref, *, d):
    x = x_ref[...]
    h = _rms(x, gn_ref[...]).astype(BF16)
    m = None
    for i, (b_ref, w_ref) in enumerate(((og_ref, wg_ref), (or_ref, wr_ref), (os_ref, ws_ref))):
        gate = _sigmoid(_dot(h, wt_ref[:, i * d:(i + 1) * d]))
        term = gate * _dot(b_ref[...], w_ref[...])
        m = term if m is None else m + term
    o_ref[...] = x + _dot(m.astype(BF16), wo_ref[...])


def _merge(x, og, orw, os5, gn, wt, wg, wr, ws, wo):
    m, d = x.shape
    tm = _pick_block(m, 16, 512)
    return pl.pallas_call(
        functools.partial(_merge_body, d=d),
        grid=(m // tm,),
        in_specs=[_row_spec(tm, d), _row_spec(tm, GLA_V), _row_spec(tm, RWKV_W), _row_spec(tm, S5_WIDTH),
                  _const_spec((1, d))] + [_const_spec(w.shape) for w in (wt, wg, wr, ws, wo)],
        out_specs=_row_spec(tm, d),
        out_shape=jax.ShapeDtypeStruct((m, d), F32),
        compiler_params=_params(("parallel",)),
        name="merge",
    )(x, og, orw, os5, gn, wt, wg, wr, ws, wo)


def _ffn_body(x_ref, g_ref, w1_ref, w3_ref, w2_ref, o_ref, *, dff, cf):
    x = x_ref[...]
    h = _rms(x, g_ref[...]).astype(BF16)
    acc = x
    for c0 in range(0, dff, cf):
        a = _dot(h, w1_ref[:, c0:c0 + cf])
        b = _dot(h, w3_ref[:, c0:c0 + cf])
        acc = acc + _dot((a * _sigmoid(a) * b).astype(BF16), w2_ref[c0:c0 + cf, :])
    o_ref[...] = acc


def _ffn(x, g, w1, w3, w2):
    m, d = x.shape
    tm = _pick_block(m, SUBLANES, 1024)
    dff = w1.shape[1]
    cf = _pick_block(dff, LANES, 512)
    return pl.pallas_call(
        functools.partial(_ffn_body, dff=dff, cf=cf),
        grid=(m // tm,),
        in_specs=[_row_spec(tm, d), _const_spec((1, d)), _const_spec(w1.shape), _const_spec(w3.shape),
                  _const_spec(w2.shape)],
        out_specs=_row_spec(tm, d),
        out_shape=jax.ShapeDtypeStruct((m, d), F32),
        compiler_params=_params(("parallel",)),
        name="ffn",
    )(x, g, w1, w3, w2)


def _final_norm_body(x_ref, g_ref, o_ref):
    o_ref[...] = _rms(x_ref[...], g_ref[...])


def _final_norm(x, g):
    m, d = x.shape
    tm = _pick_block(m, SUBLANES, 1024)
    return pl.pallas_call(
        _final_norm_body,
        grid=(m // tm,),
        in_specs=[_row_spec(tm, d), _const_spec((1, d))],
        out_specs=_row_spec(tm, d),
        out_shape=jax.ShapeDtypeStruct((m, d), F32),
        compiler_params=_params(("parallel",)),
        name="final_norm",
    )(x, g)


def _block_ones(n, blk, dtype):
    i = jnp.arange(n) // blk
    return (i[:, None] == i[None, :]).astype(dtype)


def _gla_consts():
    c = GLA_SUB
    i = jnp.arange(GLA_TILE)
    same = i[:, None] // c == i[None, :] // c
    tri = ((i[:, None] >= i[None, :]) & same).astype(F32)
    tot = same.astype(F32)
    first = (jnp.arange(GLA_TILE // c)[:, None] * c == i[None, :]).astype(F32)
    segb = (jnp.arange(GLA_QK)[:, None] // GLA_DK == jnp.arange(GLA_V)[None, :] // GLA_DV).astype(BF16)
    sel = (jnp.arange(16)[:, None] // 3 == jnp.arange(3 * GLA_DV)[None, :] // GLA_DV).astype(BF16)
    return (tri, tot, first, segb), sel


def _block_diag_slabs(p, nslab):
    g, a, b = p.shape
    gs = g // nslab
    eye = jnp.eye(gs, dtype=p.dtype)
    p = p.reshape(nslab, gs, a, b)
    return (p[:, :, :, None, :] * eye[None, :, None, :, None]).reshape(nslab, gs * a, gs * b)


def kernel(x_prompt, x_sample, state_gla, state_rwkv, state_rwkv_shift, state_s5_re, state_s5_im, meta_tokens, norm_mix, norm_ffn, w_in, gla_gate_w2, gla_gate_b, gla_norm, rwkv_mu, rwkv_w0, rwkv_w2, rwkv_a0, rwkv_a2, rwkv_g2, rwkv_k_k, rwkv_k_a, rwkv_r_k, rwkv_ln_w, rwkv_ln_b, s5_a_re, s5_a_im, s5_log_dt, s5_b_re, s5_b_im, s5_c_re, s5_c_im, s5_d, s5_glu_w, s5_glu_b, w_br_gla, w_br_rwkv, w_br_s5, w_out, ffn_w1, ffn_w3, ffn_w2, final_norm):
    nb, seq, d = x_prompt.shape
    ns = x_sample.shape[0]
    assert x_sample.shape[1] == 1
    depth = w_in.shape[0]
    t = seq + N_META
    tp = -(-t // LANES) * LANES
    mp = nb * tp
    assert nb * RWKV_HEADS == CHAINS and (ns * RWKV_HEADS) % CHAINS == 0

    meta = jnp.broadcast_to(meta_tokens.astype(x_prompt.dtype)[None], (nb, N_META, d))
    xp = jnp.concatenate([meta, x_prompt, jnp.zeros((nb, tp - t, d), x_prompt.dtype)], axis=1).reshape(mp, d)
    xs = x_sample.reshape(ns, d)

    gla_unroll = 3 if t % (3 * GLA_SUB) == 0 else 1
    gla_consts, sel = _gla_consts()
    bd = _block_ones(RWKV_W, RWKV_N, BF16)
    o_q, o_z = 0, 2 * GLA_QK + 2 * GLA_V
    o_r = o_z + GLA_GATE_RANK
    o_s = o_r + RWKV_COLS
    o_t = o_s + S5_WIDTH

    new_p = [[] for _ in range(5)]
    new_s = [[] for _ in range(5)]
    for l in range(depth):
        wl = w_in[l]
        wg = jnp.concatenate([wl[:, o_q:o_r], jnp.zeros((d, GLA_ZPAD - GLA_GATE_RANK), F32)], axis=1).astype(BF16)
        w_groups = (wg, wl[:, o_r:o_s].astype(BF16), wl[:, o_s:o_t].astype(BF16))
        w_gates = wl[:, o_t:].astype(BF16)
        pg_p, pr_p, ps_p = _inproj(xp, norm_mix[l][None], w_groups)
        pg_s, pr_s, ps_s = _inproj(xs, norm_mix[l][None], w_groups)

        w2p = jnp.concatenate([gla_gate_w2[l], jnp.zeros((GLA_ZPAD - GLA_GATE_RANK, GLA_QK), F32)], axis=0).astype(BF16)
        gb = gla_gate_b[l][None]
        gn = gla_norm[l][None]
        og_p, gla_p = _gla_state(*_gla_intra(pg_p, w2p, gb, gn, gla_consts), nb, t, tp, gla_unroll)
        gla_p = jnp.swapaxes(gla_p[:, :, :, :GLA_DK], 2, 3)
        og_s, gla_s = _gla_sample(pg_s, state_gla[l].reshape(ns, GLA_QK, GLA_DV), w2p, gb, gn, sel)
        new_p[0].append(gla_p.reshape(nb, GLA_HEADS, GLA_DK, GLA_DV))
        new_s[0].append(gla_s.reshape(ns, GLA_HEADS, GLA_DK, GLA_DV))

        wlow = jnp.zeros((RWKV_LOW, 3 * RWKV_W), F32)
        wlow = wlow.at[0:RWKV_W_RANK, 0:RWKV_W].set(rwkv_w2[l])
        wlow = wlow.at[RWKV_W_RANK:RWKV_W_RANK + RWKV_A_RANK, RWKV_W:2 * RWKV_W].set(rwkv_a2[l])
        wlow = wlow.at[RWKV_W_RANK + RWKV_A_RANK:, 2 * RWKV_W:].set(rwkv_g2[l])
        prep_consts = (rwkv_mu[l][None], rwkv_w0[l][None], rwkv_a0[l][None], rwkv_k_k[l][None], rwkv_k_a[l][None],
                       rwkv_r_k[l].reshape(1, RWKV_W), wlow.astype(BF16), bd)
        post_consts = (rwkv_ln_w[l][None], rwkv_ln_b[l][None], bd)
        (bonus, g_), tr = _rwkv_prep_seq(pr_p, prep_consts, nb, tp)
        chain = _to_chain(tr[:3], nb, tp) + _to_chain(tr[3:], nb, tp)
        yt, sf = _rwkv_scan(*chain, _state_to_chains(jnp.zeros((nb, RWKV_HEADS, RWKV_N, RWKV_N), F32), nb), t)
        orw_p = _rwkv_post(_from_chain(yt.reshape(tp * HALF_N, LANES), nb, tp), bonus, g_, *post_consts)
        new_p[1].append(_state_from_chains(sf, nb))
        new_p[2].append(pr_p.reshape(nb, tp, RWKV_COLS)[:, t - 1])
        *scan_ops, bonus, g_ = _rwkv_prep_step(pr_s, state_rwkv_shift[l], prep_consts)
        chain = [_sample_to_chain(a, ns) for a in scan_ops]
        yt, sf = _rwkv_scan(*chain, _state_to_chains(state_rwkv[l], ns), 1)
        orw_s = _rwkv_post(_sample_from_chain(yt, ns), bonus, g_, *post_consts)
        new_s[1].append(_state_from_chains(sf, ns))
        new_s[2].append(pr_s)

        lre = s5_a_re[l].reshape(1, S5_N)
        lim = s5_a_im[l].reshape(1, S5_N)
        ldt = jnp.broadcast_to(s5_log_dt[l][:, None], (S5_GROUPS, S5_STATE)).reshape(1, S5_N)
        bre = _block_diag_slabs(jnp.swapaxes(s5_b_re[l], 1, 2), S5_SLABS)
        bim = _block_diag_slabs(jnp.swapaxes(s5_b_im[l], 1, 2), S5_SLABS)
        cre = _block_diag_slabs(jnp.swapaxes(s5_c_re[l], 1, 2), S5_SLABS).astype(BF16)
        cim = _block_diag_slabs(jnp.swapaxes(s5_c_im[l], 1, 2), S5_SLABS).astype(BF16)
        s5_args = (lre, lim, ldt, bre, bim, cre, cim, s5_d[l][None], s5_glu_w[l].astype(BF16), s5_glu_b[l][None])
        os_p, xf_p = _s5(ps_p.reshape(nb, tp, S5_WIDTH), jnp.zeros((nb, 2 * S5_N), F32), *s5_args, nb, t)
        os_p = os_p.reshape(mp, S5_WIDTH)
        x0_s = jnp.concatenate([state_s5_re[l].reshape(ns, S5_N), state_s5_im[l].reshape(ns, S5_N)], axis=1)
        os_s, xf_s = _s5(ps_s.reshape(ns, 1, S5_WIDTH), x0_s, *s5_args, ns, 1)
        os_s = os_s.reshape(ns, S5_WIDTH)
        new_p[3].append(xf_p[:, :S5_N].reshape(nb, S5_GROUPS, S5_STATE))
        new_p[4].append(xf_p[:, S5_N:].reshape(nb, S5_GROUPS, S5_STATE))
        new_s[3].append(xf_s[:, :S5_N].reshape(ns, S5_GROUPS, S5_STATE))
        new_s[4].append(xf_s[:, S5_N:].reshape(ns, S5_GROUPS, S5_STATE))

        w_merge = (w_br_gla[l].astype(BF16), w_br_rwkv[l].astype(BF16), w_br_s5[l].astype(BF16), w_out[l].astype(BF16))
        w_ffn = (norm_ffn[l][None], ffn_w1[l].astype(BF16), ffn_w3[l].astype(BF16), ffn_w2[l].astype(BF16))
        xp = _ffn(_merge(xp, og_p, orw_p, os_p, norm_mix[l][None], w_gates, *w_merge), *w_ffn)
        xs = _ffn(_merge(xs, og_s, orw_s, os_s, norm_mix[l][None], w_gates, *w_merge), *w_ffn)

    y_prompt = _final_norm(xp, final_norm[None]).reshape(nb, tp, d)[:, N_META:t]
    y_sample = _final_norm(xs, final_norm[None]).reshape(ns, 1, d)
    outs_p = [jnp.stack(a, axis=0) for a in new_p]
    outs_s = [jnp.stack(a, axis=0) for a in new_s]
    return (y_prompt, y_sample, *outs_p, *outs_s)
```

```python
import functools
import math

import jax
import jax.numpy as jnp
from jax import lax
from jax.experimental import pallas as pl
from jax.experimental.pallas import tpu as pltpu

F32 = jnp.float32
BF16 = jnp.bfloat16
HIGHEST = lax.Precision.HIGHEST

N_META = 16
NORM_EPS = 1e-6
GLA_HEADS = 4
GLA_DK = 64
GLA_DV = 128
GLA_QK = GLA_HEADS * GLA_DK
GLA_V = GLA_HEADS * GLA_DV
GLA_GATE_RANK = 16
GLA_TAU = 16.0
GLA_SUB = 16
RWKV_HEADS = 8
RWKV_N = 64
RWKV_W = RWKV_HEADS * RWKV_N
RWKV_W_RANK = 64
RWKV_A_RANK = 64
RWKV_G_RANK = 128
RWKV_LOW = RWKV_W_RANK + RWKV_A_RANK + RWKV_G_RANK
RWKV_COLS = 3 * RWKV_W + RWKV_LOW
RWKV_GN_EPS = 64e-5
S5_GROUP = 16
S5_GROUPS = 32
S5_WIDTH = S5_GROUPS * S5_GROUP
S5_STATE = 64
S5_N = S5_GROUPS * S5_STATE

LANES = 128
SUBLANES = 8
CHAINS = LANES // 2
HALF_N = RWKV_N // 2
VTILES = HALF_N // SUBLANES
S5_SLABS = S5_WIDTH // LANES
S5_SLAB_N = S5_N // S5_SLABS
GLA_ZPAD = LANES
GLA_COLS = 2 * GLA_QK + 2 * GLA_V + GLA_ZPAD
VMEM_LIMIT = 56 << 20


def _params(sem):
    return pltpu.CompilerParams(dimension_semantics=sem, vmem_limit_bytes=VMEM_LIMIT)


def _const_spec(shape):
    nd = len(shape)
    return pl.BlockSpec(shape, lambda *_: (0,) * nd, pipeline_mode=pl.Buffered(1))


def _row_spec(tm, cols):
    return pl.BlockSpec((tm, cols), lambda i: (i, 0))


def _pick_block(n, mult, cap):
    best = None
    for d in range(mult, min(n, cap) + 1, mult):
        if n % d == 0:
            best = d
    assert best is not None, (n, mult, cap)
    return best


def _sigmoid(x):
    return 1.0 / (1.0 + jnp.exp(-x))


def _softplus(x):
    return jnp.maximum(x, 0.0) + jnp.log1p(jnp.exp(-jnp.abs(x)))


def _rms(x, g):
    return x * lax.rsqrt(jnp.mean(x * x, axis=-1, keepdims=True) + NORM_EPS) * g


def _dot(a, b):
    return jnp.dot(a, b, preferred_element_type=F32)


def _dot_t0(a, b):
    return lax.dot_general(a, b, (((0,), (0,)), ((), ())), preferred_element_type=F32)


def _split3_rows(x):
    hi = x.astype(BF16).astype(F32)
    r1 = x - hi
    mid = r1.astype(BF16).astype(F32)
    lo = (r1 - mid).astype(BF16).astype(F32)
    return [hi, mid, lo]


def _inproj_body(x_ref, g_ref, *refs):
    n = len(refs) // 2
    h = _rms(x_ref[...], g_ref[...]).astype(BF16)
    for w_ref, o_ref in zip(refs[:n], refs[n:]):
        o_ref[...] = _dot(h, w_ref[...])


def _inproj(x, g, ws):
    m, d = x.shape
    tm = _pick_block(m, SUBLANES, 512)
    widths = [w.shape[1] for w in ws]
    return pl.pallas_call(
        _inproj_body,
        grid=(m // tm,),
        in_specs=[_row_spec(tm, d), _const_spec((1, d))] + [_const_spec(w.shape) for w in ws],
        out_specs=[_row_spec(tm, c) for c in widths],
        out_shape=[jax.ShapeDtypeStruct((m, c), F32) for c in widths],
        compiler_params=_params(("parallel",)),
        name="inproj",
    )(x, g, *ws)


def _log_sigmoid(x):
    return jnp.minimum(x, 0.0) - jnp.log1p(jnp.exp(-jnp.abs(x)))


def _gla_finish(o, g, gn):
    parts = []
    for h in range(GLA_HEADS):
        oh = o[:, h * GLA_DV:(h + 1) * GLA_DV]
        parts.append(oh * lax.rsqrt(jnp.mean(oh * oh, axis=-1, keepdims=True) + NORM_EPS))
    on = jnp.concatenate(parts, axis=1)
    return on * gn * (g * _sigmoid(g))


def _gla_log_decay(z, w2_ref, gb_ref):
    zz = _dot(z.astype(BF16), w2_ref[...]) + gb_ref[...]
    return _log_sigmoid(zz) * (1.0 / GLA_TAU)


GLA_TILE = LANES


def _pad_heads(x):
    zero = jnp.zeros((x.shape[0], LANES - GLA_DK), x.dtype)
    parts = []
    for h in range(GLA_HEADS):
        parts += [x[:, h * GLA_DK:(h + 1) * GLA_DK], zero]
    return jnp.concatenate(parts, axis=1)


def _gla_intra_body(p_ref, w2_ref, gb_ref, gn_ref, tri_ref, tot_ref, first_ref, segb_ref,
                    oi_ref, qe_ref, kd_ref, vb_ref, gate_ref, ebl_ref):
    c = GLA_SUB
    q = p_ref[:, 0:GLA_QK]
    k = p_ref[:, GLA_QK:2 * GLA_QK]
    v = p_ref[:, 2 * GLA_QK:2 * GLA_QK + GLA_V]
    g = p_ref[:, 2 * GLA_QK + GLA_V:2 * GLA_QK + 2 * GLA_V]
    z = p_ref[:, 2 * GLA_QK + 2 * GLA_V:GLA_COLS]
    la = _gla_log_decay(z, w2_ref, gb_ref)
    bc = jnp.dot(tri_ref[...], la, precision=HIGHEST, preferred_element_type=F32)
    bl = jnp.dot(tot_ref[...], la, precision=HIGHEST, preferred_element_type=F32)
    qs = q * (GLA_DK ** -0.5)
    qe_ref[...] = _pad_heads(qs * jnp.exp(bc)).astype(BF16)
    kd_ref[...] = _pad_heads(k * jnp.exp(bl - bc)).astype(BF16)
    ebl_ref[...] = jnp.dot(first_ref[...], _pad_heads(jnp.exp(bl)), precision=HIGHEST, preferred_element_type=F32)
    vb_ref[...] = v.astype(BF16)
    gate_ref[...] = gn_ref[...] * (g * _sigmoid(g))
    row_id = lax.broadcasted_iota(jnp.int32, (c, GLA_QK), 0)
    nchunk = p_ref.shape[0] // c
    per_dot = 4
    for c0 in range(0, nchunk, per_dot):
        terms = []
        for u in range(c0, c0 + per_dot):
            sl = slice(u * c, (u + 1) * c)
            for jj in range(c):
                e = jnp.exp(bc[sl] - bc[u * c + jj:u * c + jj + 1, :])
                terms.append(jnp.where(row_id >= jj, qs[sl] * k[u * c + jj:u * c + jj + 1, :] * e, 0.0))
        pw = _dot(jnp.concatenate(terms, axis=0).astype(BF16), segb_ref[...])
        for i, u in enumerate(range(c0, c0 + per_dot)):
            base = i * c * c
            acc = pw[base:base + c] * v[u * c:u * c + 1, :]
            for jj in range(1, c):
                acc = acc + pw[base + jj * c:base + (jj + 1) * c] * v[u * c + jj:u * c + jj + 1, :]
            oi_ref[u * c:(u + 1) * c, :] = acc


def _gla_intra(pg, w2p, gb, gn, consts):
    m = pg.shape[0]
    tm = GLA_TILE
    assert m % tm == 0
    tri, tot, first, segb = consts
    nck = tm // GLA_SUB
    wide = GLA_HEADS * LANES
    return pl.pallas_call(
        _gla_intra_body,
        grid=(m // tm,),
        in_specs=[_row_spec(tm, GLA_COLS), _const_spec(w2p.shape), _const_spec(gb.shape), _const_spec(gn.shape),
                  _const_spec(tri.shape), _const_spec(tot.shape), _const_spec(first.shape), _const_spec(segb.shape)],
        out_specs=[_row_spec(tm, GLA_V), _row_spec(tm, wide), _row_spec(tm, wide), _row_spec(tm, GLA_V),
                   _row_spec(tm, GLA_V), _row_spec(nck, wide)],
        out_shape=[jax.ShapeDtypeStruct((m, GLA_V), F32), jax.ShapeDtypeStruct((m, wide), BF16),
                   jax.ShapeDtypeStruct((m, wide), BF16), jax.ShapeDtypeStruct((m, GLA_V), BF16),
                   jax.ShapeDtypeStruct((m, GLA_V), F32), jax.ShapeDtypeStruct((m // GLA_SUB, wide), F32)],
        compiler_params=_params(("parallel",)),
        name="gla_intra",
    )(pg, w2p, gb, gn, tri, tot, first, segb)


def _gla_state_body(oi_ref, qe_ref, kd_ref, vb_ref, gate_ref, ebl_ref, o_ref, sf_ref, st_ref, *, nchunk, unroll):
    c = GLA_SUB
    gr = c * unroll
    ngroup = -(-nchunk // unroll)
    last_u = (nchunk - 1) % unroll

    st_ref[...] = jnp.zeros_like(st_ref)

    def group(gi, carry):
        rows = pl.ds(pl.multiple_of(gi * gr, gr), gr)
        qe = qe_ref[rows, :]
        kd = kd_ref[rows, :]
        vb = vb_ref[rows, :]
        st = [st_ref[h] for h in range(GLA_HEADS)]
        heads = [slice(h * LANES, (h + 1) * LANES) for h in range(GLA_HEADS)]
        chunks = [slice(u * c, (u + 1) * c) for u in range(unroll)]
        kv = [[_dot_t0(vb[sl, hs], kd[sl, hs]) for hs in heads] for sl in chunks]
        outs = []
        for u, sl in enumerate(chunks):
            dec = ebl_ref[gi * unroll + u]
            parts = []
            for h, hs in enumerate(heads):
                parts.append(lax.dot_general(qe[sl, hs], st[h].astype(BF16), (((1,), (1,)), ((), ())),
                                             preferred_element_type=F32))
                st[h] = st[h] * dec[:, hs] + kv[u][h]
                if u == last_u:
                    sf_ref[0, h] = st[h]
            outs.append(jnp.concatenate(parts, axis=1))
        o = (outs[0] if unroll == 1 else jnp.concatenate(outs, axis=0)) + oi_ref[rows, :]
        normed = []
        for h in range(GLA_HEADS):
            oh = o[:, h * GLA_DV:(h + 1) * GLA_DV]
            normed.append(oh * lax.rsqrt(jnp.mean(oh * oh, axis=-1, keepdims=True) + NORM_EPS))
        o_ref[rows, :] = (jnp.concatenate(normed, axis=1) * gate_ref[rows, :]).astype(o_ref.dtype)
        for h in range(GLA_HEADS):
            st_ref[h] = st[h]
        return carry

    lax.fori_loop(0, ngroup, group, 0)
    t = nchunk * c
    if t < o_ref.shape[0]:
        o_ref[t:, :] = jnp.zeros((o_ref.shape[0] - t, GLA_V), o_ref.dtype)


def _gla_state(oi, qe, kd, vb, gate, ebl, nb, t, tp):
    unroll = GLA_TILE // GLA_SUB
    assert t % GLA_SUB == 0 and tp % GLA_TILE == 0 and tp >= t
    wide = GLA_HEADS * LANES
    nck = tp // GLA_SUB
    seq = lambda cols: pl.BlockSpec((tp, cols), lambda b: (b, 0))
    return pl.pallas_call(
        functools.partial(_gla_state_body, nchunk=t // GLA_SUB, unroll=unroll),
        grid=(nb,),
        in_specs=[seq(GLA_V), seq(wide), seq(wide), seq(GLA_V), seq(GLA_V),
                  pl.BlockSpec((nck, 1, wide), lambda b: (b, 0, 0))],
        out_specs=[seq(GLA_V), pl.BlockSpec((1, GLA_HEADS, GLA_DV, LANES), lambda b: (b, 0, 0, 0))],
        out_shape=[jax.ShapeDtypeStruct((nb * tp, GLA_V), BF16),
                   jax.ShapeDtypeStruct((nb, GLA_HEADS, GLA_DV, LANES), F32)],
        scratch_shapes=[pltpu.VMEM((GLA_HEADS, GLA_DV, LANES), F32)],
        compiler_params=_params(("parallel",)),
        name="gla_state",
    )(oi, qe, kd, vb, gate, ebl.reshape(nb * nck, 1, wide))


def _gla_sample_body(p_ref, s0_ref, w2_ref, gb_ref, gn_ref, sel_ref, o_ref, s_ref, *, nrow):
    q = p_ref[:, 0:GLA_QK]
    k = p_ref[:, GLA_QK:2 * GLA_QK]
    v = p_ref[:, 2 * GLA_QK:2 * GLA_QK + GLA_V]
    g = p_ref[:, 2 * GLA_QK + GLA_V:2 * GLA_QK + 2 * GLA_V]
    z = p_ref[:, 2 * GLA_QK + 2 * GLA_V:GLA_COLS]
    a = jnp.exp(_gla_log_decay(z, w2_ref, gb_ref))
    qs = q * (GLA_DK ** -0.5)
    out_rows = []
    for i in range(nrow):
        pieces = _split3_rows(a[i:i + 1]) + _split3_rows(k[i:i + 1]) + _split3_rows(qs[i:i + 1])
        ef = jnp.concatenate(pieces + [jnp.zeros((16 - 9, GLA_QK), F32)], axis=0)
        cb = _dot_t0(ef.astype(BF16), sel_ref[...])
        ab = cb[:, 0:GLA_DV]
        kb = cb[:, GLA_DV:2 * GLA_DV]
        qb = cb[:, 2 * GLA_DV:3 * GLA_DV]
        vb = jnp.concatenate(
            [jnp.broadcast_to(v[i:i + 1, h * GLA_DV:(h + 1) * GLA_DV], (GLA_DK, GLA_DV)) for h in range(GLA_HEADS)],
            axis=0)
        s = ab * s0_ref[i] + kb * vb
        s_ref[i] = s
        op = qb * s
        out_rows.append(jnp.concatenate(
            [jnp.sum(op[h * GLA_DK:(h + 1) * GLA_DK], axis=0, keepdims=True) for h in range(GLA_HEADS)], axis=1))
    o = jnp.concatenate(out_rows, axis=0)
    o_ref[...] = _gla_finish(o, g, gn_ref[...]).astype(o_ref.dtype)


def _gla_sample(pg, s0, w2p, gb, gn, sel):
    ns = pg.shape[0]
    nrow = 16
    assert ns % nrow == 0
    return pl.pallas_call(
        functools.partial(_gla_sample_body, nrow=nrow),
        grid=(ns // nrow,),
        in_specs=[_row_spec(nrow, GLA_COLS),
                  pl.BlockSpec((nrow, GLA_QK, GLA_DV), lambda i: (i, 0, 0)),
                  _const_spec(w2p.shape), _const_spec(gb.shape), _const_spec(gn.shape), _const_spec(sel.shape)],
        out_specs=[_row_spec(nrow, GLA_V), pl.BlockSpec((nrow, GLA_QK, GLA_DV), lambda i: (i, 0, 0))],
        out_shape=[jax.ShapeDtypeStruct((ns, GLA_V), BF16), jax.ShapeDtypeStruct((ns, GLA_QK, GLA_DV), F32)],
        compiler_params=_params(("parallel",)),
        name="gla_sample",
    )(pg, s0, w2p, gb, gn, sel)


def _segsum(x, bd):
    hi = x.astype(BF16)
    lo = (x - hi.astype(F32)).astype(BF16)
    return _dot(hi, bd) + _dot(lo, bd)


N_PREP_CONSTS = 8


def _rwkv_prep_math(p, prev, mu_ref, w0_ref, a0_ref, kkw_ref, kaw_ref, rk_ref, wlow_ref, bd_ref):
    xs = p + (prev - p) * mu_ref[...]
    r = xs[:, 0:RWKV_W]
    k = xs[:, RWKV_W:2 * RWKV_W]
    v = xs[:, 2 * RWKV_W:3 * RWKV_W]
    zl = xs[:, 3 * RWKV_W:RWKV_COLS]
    lane = lax.broadcasted_iota(jnp.int32, zl.shape, 1)
    zt = jnp.where(lane < RWKV_W_RANK, jnp.tanh(zl),
                   jnp.where(lane < RWKV_W_RANK + RWKV_A_RANK, zl, _sigmoid(zl)))
    low = _dot(zt.astype(BF16), wlow_ref[...])
    w_log = -_softplus(-(w0_ref[...] + low[:, 0:RWKV_W])) - 0.5
    a = _sigmoid(a0_ref[...] + low[:, RWKV_W:2 * RWKV_W])
    kk = k * kkw_ref[...]
    kk = kk * lax.rsqrt(_segsum(kk * kk, bd_ref[...]) + 1e-12)
    k = k * (1.0 + (a - 1.0) * kaw_ref[...])
    bonus = _segsum(r * k * rk_ref[...], bd_ref[...]) * v
    return (jnp.exp(-jnp.exp(w_log)), kk, kk * a, k, r, v), (bonus, low[:, 2 * RWKV_W:3 * RWKV_W])


PREP_SEQS = 4


def _rwkv_prep_seq_body(p_ref, *rest):
    consts, outs, carry_ref = rest[:N_PREP_CONSTS], rest[N_PREP_CONSTS:-1], rest[-1]
    nat, tr = outs[:2], outs[2:]

    @pl.when(pl.program_id(1) == 0)
    def _():
        carry_ref[...] = jnp.zeros_like(carry_ref)

    row = lax.broadcasted_iota(jnp.int32, p_ref.shape[1:], 0)
    for b in range(p_ref.shape[0]):
        p = p_ref[b]
        prev = jnp.where(row == 0, carry_ref[b:b + 1, :], pltpu.roll(p, 1, axis=0))
        carry_ref[b:b + 1, :] = p[p.shape[0] - 1:, :]
        scan_ops, out_ops = _rwkv_prep_math(p, prev, *consts)
        for ref, val in zip(nat, out_ops):
            ref[b] = val
        for ref, val in zip(tr, scan_ops):
            ref[b] = val.T


def _rwkv_prep_step_body(p_ref, pp_ref, *rest):
    scan_ops, out_ops = _rwkv_prep_math(p_ref[...], pp_ref[...], *rest[:N_PREP_CONSTS])
    for ref, val in zip(rest[N_PREP_CONSTS:], scan_ops + out_ops):
        ref[...] = val


def _rwkv_prep_consts_specs(consts):
    vec = _const_spec((1, RWKV_W))
    return [_const_spec((1, RWKV_COLS))] + [vec] * 5 + [_const_spec(consts[6].shape), _const_spec(consts[7].shape)]


def _rwkv_prep_seq(p, consts, nb, tp):
    tb = LANES
    ng = PREP_SEQS
    assert nb % ng == 0 and tp % tb == 0
    nat = pl.BlockSpec((ng, tb, RWKV_W), lambda i, j: (i, j, 0))
    tr = pl.BlockSpec((ng, RWKV_W, tb), lambda i, j: (i, 0, j))
    outs = pl.pallas_call(
        _rwkv_prep_seq_body,
        grid=(nb // ng, tp // tb),
        in_specs=[pl.BlockSpec((ng, tb, RWKV_COLS), lambda i, j: (i, j, 0))] + _rwkv_prep_consts_specs(consts),
        out_specs=[nat] * 2 + [tr] * 6,
        out_shape=[jax.ShapeDtypeStruct((nb, tp, RWKV_W), F32)] * 2 + [jax.ShapeDtypeStruct((nb, RWKV_W, tp), F32)] * 6,
        scratch_shapes=[pltpu.VMEM((ng, RWKV_COLS), F32)],
        compiler_params=_params(("parallel", "arbitrary")),
        name="rwkv_prep_seq",
    )(p.reshape(nb, tp, RWKV_COLS), *consts)
    return [o.reshape(nb * tp, RWKV_W) for o in outs[:2]], list(outs[2:])


def _rwkv_prep_step(p, prev, consts):
    nb = p.shape[0]
    tm = _pick_block(nb, SUBLANES, 256)
    return pl.pallas_call(
        _rwkv_prep_step_body,
        grid=(nb // tm,),
        in_specs=[_row_spec(tm, RWKV_COLS), _row_spec(tm, RWKV_COLS)] + _rwkv_prep_consts_specs(consts),
        out_specs=[_row_spec(tm, RWKV_W)] * 8,
        out_shape=[jax.ShapeDtypeStruct((nb, RWKV_W), F32)] * 8,
        compiler_params=_params(("parallel",)),
        name="rwkv_prep_step",
    )(p, prev, *consts)


def _to_chain_body(*refs, nb, narr):
    x_refs, o_refs = refs[:narr], refs[narr:]
    tc = x_refs[0].shape[2]

    def chains(x_ref, n):
        return [x_ref[b, pl.ds(n, RWKV_HEADS, stride=RWKV_N), :] for b in range(nb)]

    for i in range(narr):
        for r in range(HALF_N):
            both = jnp.concatenate(chains(x_refs[i], r) + chains(x_refs[i], HALF_N + r), axis=0)
            o_refs[i][pl.ds(r, tc, stride=HALF_N), :] = both.T


def _to_chain(xs, nb, tp):
    assert nb * RWKV_HEADS == CHAINS and tp % LANES == 0
    tc = LANES
    narr = len(xs)
    outs = pl.pallas_call(
        functools.partial(_to_chain_body, nb=nb, narr=narr),
        grid=(tp // tc,),
        in_specs=[pl.BlockSpec((nb, RWKV_W, tc), lambda j: (0, 0, j))] * narr,
        out_specs=[pl.BlockSpec((tc * HALF_N, LANES), lambda j: (j, 0))] * narr,
        out_shape=[jax.ShapeDtypeStruct((tp * HALF_N, LANES), F32)] * narr,
        compiler_params=_params(("parallel",)),
        name="rwkv_to_chain",
    )(*xs)
    return [o.reshape(1, tp, HALF_N, LANES) for o in outs]


def _from_chain_body(y_ref, o_ref, z_ref, *, nb):
    tc = o_ref.shape[1]
    for r in range(HALF_N):
        both = y_ref[pl.ds(r, tc, stride=HALF_N), :].T
        z_ref[pl.ds(r, CHAINS, stride=RWKV_N), :] = both[0:CHAINS]
        z_ref[pl.ds(HALF_N + r, CHAINS, stride=RWKV_N), :] = both[CHAINS:LANES]
    for b in range(nb):
        o_ref[b] = z_ref[b * RWKV_W:(b + 1) * RWKV_W, :].T


def _from_chain(y, nb, tp):
    tc = LANES
    out = pl.pallas_call(
        functools.partial(_from_chain_body, nb=nb),
        grid=(tp // tc,),
        in_specs=[pl.BlockSpec((tc * HALF_N, LANES), lambda j: (j, 0))],
        out_specs=pl.BlockSpec((nb, tc, RWKV_W), lambda j: (0, j, 0)),
        out_shape=jax.ShapeDtypeStruct((nb, tp, RWKV_W), F32),
        scratch_shapes=[pltpu.VMEM((nb * RWKV_W, tc), F32)],
        compiler_params=_params(("parallel",)),
        name="rwkv_from_chain",
    )(y)
    return out.reshape(nb * tp, RWKV_W)


def _rwkv_scan_body(w_ref, kk_ref, kka_ref, k_ref, r_ref, v_ref, s0_ref, yin_ref, y_ref, sf_ref, s_ref, u_ref, *, tc):
    del yin_ref
    j = pl.program_id(1)
    key_refs = (w_ref, kk_ref, kka_ref, k_ref, r_ref)
    low_half = lax.broadcasted_iota(jnp.int32, (HALF_N, LANES), 1) < CHAINS

    def spread(t, slot):
        for i, ref in enumerate(key_refs):
            x = ref[0, t]
            sw = pltpu.roll(x, CHAINS, axis=1)
            u_ref[slot, i, 0:HALF_N, :] = jnp.where(low_half, x, sw)
            u_ref[slot, i, HALF_N:RWKV_N, :] = jnp.where(low_half, sw, x)

    @pl.when(j == 0)
    def _():
        s_ref[...] = s0_ref[0]

    spread(0, 0)

    nsum = 4

    def total(parts):
        return (parts[0] + parts[1]) + (parts[2] + parts[3])

    def token(t, carry):
        slot = t % 2
        acc = [[None] * nsum for _ in range(VTILES)]
        for key in range(RWKV_N):
            kk = u_ref[slot, 1, key:key + 1, :]
            for i in range(VTILES):
                term = s_ref[i, key] * kk
                acc[i][key % nsum] = term if key < nsum else acc[i][key % nsum] + term
        sk = [total(a) for a in acc]
        vv = [v_ref[0, t, i * SUBLANES:(i + 1) * SUBLANES, :] for i in range(VTILES)]
        yacc = [[None] * nsum for _ in range(VTILES)]
        for key in range(RWKV_N):
            w = u_ref[slot, 0, key:key + 1, :]
            kka = u_ref[slot, 2, key:key + 1, :]
            k = u_ref[slot, 3, key:key + 1, :]
            r = u_ref[slot, 4, key:key + 1, :]
            for i in range(VTILES):
                sn = s_ref[i, key] * w - sk[i] * kka + vv[i] * k
                s_ref[i, key] = sn
                term = sn * r
                yacc[i][key % nsum] = term if key < nsum else yacc[i][key % nsum] + term
        for i in range(VTILES):
            y_ref[0, t, i * SUBLANES:(i + 1) * SUBLANES, :] = total(yacc[i])
        spread(jnp.minimum(t + 1, tc - 1), 1 - slot)
        return carry

    lax.fori_loop(0, tc, token, 0)

    @pl.when(j == pl.num_programs(1) - 1)
    def _():
        sf_ref[0] = s_ref[...]


def _rwkv_scan(w, kk, kka, k, r, v, s0, t):
    g, tp = w.shape[0], w.shape[1]
    tc = _pick_block(t, 1, 96)
    tok = pl.BlockSpec((1, tc, HALF_N, LANES), lambda i, j: (i, j, 0, 0))
    st = pl.BlockSpec((1, VTILES, RWKV_N, SUBLANES, LANES), lambda i, j: (i, 0, 0, 0, 0))
    y0 = jnp.zeros((g, tp, HALF_N, LANES), F32)
    return pl.pallas_call(
        functools.partial(_rwkv_scan_body, tc=tc),
        grid=(g, t // tc),
        in_specs=[tok] * 6 + [st, pl.BlockSpec(memory_space=pl.ANY)],
        out_specs=[tok, st],
        out_shape=[jax.ShapeDtypeStruct((g, tp, HALF_N, LANES), F32),
                   jax.ShapeDtypeStruct((g, VTILES, RWKV_N, SUBLANES, LANES), F32)],
        scratch_shapes=[pltpu.VMEM((VTILES, RWKV_N, SUBLANES, LANES), F32),
                        pltpu.VMEM((2, 5, RWKV_N, LANES), F32)],
        input_output_aliases={7: 0},
        compiler_params=_params(("parallel", "arbitrary")),
        name="rwkv_scan",
    )(w, kk, kka, k, r, v, s0, y0)


def _rwkv_post_body(y_ref, bonus_ref, g_ref, lnw_ref, lnb_ref, bd_ref, o_ref):
    y = y_ref[...]
    bd = bd_ref[...]
    inv_n = 1.0 / RWKV_N
    yc = y - _segsum(y, bd) * inv_n
    var = _segsum(yc * yc, bd) * inv_n
    yn = yc * lax.rsqrt(var + RWKV_GN_EPS) * lnw_ref[...] + lnb_ref[...]
    o_ref[...] = ((yn + bonus_ref[...]) * g_ref[...]).astype(o_ref.dtype)


def _rwkv_post(y, bonus, g, lnw, lnb, bd):
    m = y.shape[0]
    tm = _pick_block(m, 16, 1024)
    vec = _const_spec((1, RWKV_W))
    return pl.pallas_call(
        _rwkv_post_body,
        grid=(m // tm,),
        in_specs=[_row_spec(tm, RWKV_W)] * 3 + [vec, vec, _const_spec(bd.shape)],
        out_specs=_row_spec(tm, RWKV_W),
        out_shape=jax.ShapeDtypeStruct((m, RWKV_W), BF16),
        compiler_params=_params(("parallel",)),
        name="rwkv_post",
    )(y, bonus, g, lnw, lnb, bd)


def _sample_to_chain(x, nb):
    bg = CHAINS // RWKV_HEADS
    g = nb // bg
    x = x.reshape(g, bg, RWKV_HEADS, 2, HALF_N).transpose(0, 4, 3, 1, 2)
    return x.reshape(g, 1, HALF_N, LANES)


def _sample_from_chain(y, nb):
    bg = CHAINS // RWKV_HEADS
    g = nb // bg
    y = y.reshape(g, HALF_N, 2, bg, RWKV_HEADS).transpose(0, 3, 4, 2, 1)
    return y.reshape(nb, RWKV_W)


def _state_to_chains(s, nb):
    bg = CHAINS // RWKV_HEADS
    g = nb // bg
    s = s.reshape(g, bg, RWKV_HEADS, 2, VTILES, SUBLANES, RWKV_N).transpose(0, 4, 6, 5, 3, 1, 2)
    return s.reshape(g, VTILES, RWKV_N, SUBLANES, LANES)


def _state_from_chains(s, nb):
    bg = CHAINS // RWKV_HEADS
    g = nb // bg
    s = s.reshape(g, VTILES, RWKV_N, SUBLANES, 2, bg, RWKV_HEADS).transpose(0, 5, 6, 4, 1, 3, 2)
    return s.reshape(nb, RWKV_HEADS, RWKV_N, RWKV_N)


def _s5_body(u_ref, x0_ref, lre_ref, lim_ref, ldt_ref, bre_ref, bim_ref, cre_ref, cim_ref, d_ref, gw_ref, gbias_ref,
             oin_ref, o_ref, xf_ref, st_ref, ab_ref, bbar_ref, bu_ref, xs_ref, ut_ref, yt_ref, *, nrow, tc):
    del oin_ref
    j = pl.program_id(0)
    for s in range(S5_SLABS):
        lanes = slice(s * LANES, (s + 1) * LANES)
        if tc == 1:
            ut_ref[s] = u_ref[:, 0, lanes]
        else:
            for b in range(nrow):
                ut_ref[s, pl.ds(b, tc, stride=nrow), :] = u_ref[b, :, lanes]

    @pl.when(j == 0)
    def _():
        dt = jnp.exp(ldt_ref[...])
        lre = lre_ref[...]
        lim = lim_ref[...]
        mag = jnp.exp(lre * dt)
        are = mag * jnp.cos(lim * dt)
        aim = mag * jnp.sin(lim * dt)
        den = lre * lre + lim * lim
        nr = are - 1.0
        cre = (nr * lre + aim * lim) / den
        cim = (aim * lre - nr * lim) / den
        ab_ref[0:1, :] = are
        ab_ref[1:2, :] = aim
        for s in range(S5_SLABS):
            sl = slice(s * S5_SLAB_N, (s + 1) * S5_SLAB_N)
            bre = bre_ref[s]
            bim = bim_ref[s]
            bbar_ref[s, :, 0:S5_SLAB_N] = (bre * cre[:, sl] - bim * cim[:, sl]).astype(BF16)
            bbar_ref[s, :, S5_SLAB_N:2 * S5_SLAB_N] = (bre * cim[:, sl] + bim * cre[:, sl]).astype(BF16)
        st_ref[...] = x0_ref[...]

    for s in range(S5_SLABS):
        bu = _dot(ut_ref[s].astype(BF16), bbar_ref[s])
        bu_ref[:, s * S5_SLAB_N:(s + 1) * S5_SLAB_N] = bu[:, 0:S5_SLAB_N]
        bu_ref[:, S5_N + s * S5_SLAB_N:S5_N + (s + 1) * S5_SLAB_N] = bu[:, S5_SLAB_N:2 * S5_SLAB_N]
    are = ab_ref[0:1, :]
    aim = ab_ref[1:2, :]

    def token(t, carry):
        rows = pl.ds(pl.multiple_of(t * nrow, nrow), nrow)
        xr = st_ref[:, 0:S5_N]
        xi = st_ref[:, S5_N:2 * S5_N]
        nr = are * xr - aim * xi + bu_ref[rows, 0:S5_N]
        ni = are * xi + aim * xr + bu_ref[rows, S5_N:2 * S5_N]
        st_ref[:, 0:S5_N] = nr
        st_ref[:, S5_N:2 * S5_N] = ni
        xs_ref[rows, 0:S5_N] = nr
        xs_ref[rows, S5_N:2 * S5_N] = ni
        return carry

    lax.fori_loop(0, tc, token, 0)

    ys = []
    for s in range(S5_SLABS):
        sl = slice(s * S5_SLAB_N, (s + 1) * S5_SLAB_N)
        sli = slice(S5_N + s * S5_SLAB_N, S5_N + (s + 1) * S5_SLAB_N)
        ys.append(_dot(xs_ref[:, sl].astype(BF16), cre_ref[s]) - _dot(xs_ref[:, sli].astype(BF16), cim_ref[s]))
    ut = jnp.concatenate([ut_ref[s] for s in range(S5_SLABS)], axis=1)
    y = jnp.concatenate(ys, axis=1) + d_ref[...] * ut
    y = 0.5 * y * (1.0 + jnp.tanh(math.sqrt(2.0 / math.pi) * (y + 0.044715 * (y * y * y))))
    gate = _dot(y.astype(BF16), gw_ref[...]) + gbias_ref[...]
    y = y * _sigmoid(gate)
    if tc == 1:
        o_ref[:, 0, :] = y.astype(o_ref.dtype)
    else:
        for s in range(S5_SLABS):
            yt_ref[s] = y[:, s * LANES:(s + 1) * LANES]
        for b in range(nrow):
            o_ref[b] = jnp.concatenate([yt_ref[s, pl.ds(b, tc, stride=nrow), :] for s in range(S5_SLABS)],
                                       axis=1).astype(o_ref.dtype)

    @pl.when(j == pl.num_programs(0) - 1)
    def _():
        xf_ref[...] = st_ref[...]


def _s5(u, x0, lre, lim, ldt, bre, bim, cre, cim, d, gw, gbias, nrow, t):
    tp = u.shape[1]
    tc = _pick_block(t, 16, 64) if t > 1 else 1
    rows = tc * nrow
    vec = _const_spec((1, S5_N))
    o0 = jnp.zeros((nrow, tp, S5_WIDTH), BF16)
    seq = pl.BlockSpec((nrow, tc, S5_WIDTH), lambda i: (0, i, 0))
    return pl.pallas_call(
        functools.partial(_s5_body, nrow=nrow, tc=tc),
        grid=(t // tc,),
        in_specs=[seq, _const_spec((nrow, 2 * S5_N)), vec, vec, vec,
                  _const_spec(bre.shape), _const_spec(bim.shape), _const_spec(cre.shape), _const_spec(cim.shape),
                  _const_spec((1, S5_WIDTH)), _const_spec(gw.shape), _const_spec((1, S5_WIDTH)),
                  pl.BlockSpec(memory_space=pl.ANY)],
        out_specs=[seq, pl.BlockSpec((nrow, 2 * S5_N), lambda i: (0, 0))],
        out_shape=[jax.ShapeDtypeStruct((nrow, tp, S5_WIDTH), BF16), jax.ShapeDtypeStruct((nrow, 2 * S5_N), F32)],
        scratch_shapes=[pltpu.VMEM((nrow, 2 * S5_N), F32), pltpu.VMEM((SUBLANES, S5_N), F32),
                        pltpu.VMEM((S5_SLABS, LANES, 2 * S5_SLAB_N), BF16), pltpu.VMEM((rows, 2 * S5_N), F32),
                        pltpu.VMEM((rows, 2 * S5_N), F32), pltpu.VMEM((S5_SLABS, rows, LANES), F32),
                        pltpu.VMEM((S5_SLABS, rows, LANES), F32)],
        input_output_aliases={12: 0},
        compiler_params=_params(("arbitrary",)),
        name="s5",
    )(u, x0, lre, lim, ldt, bre, bim, cre, cim, d, gw, gbias, o0)


def _merge_body(x_ref, og_ref, or_ref, os_ref, gn_ref, wt_ref, wg_ref, wr_ref, ws_ref, wo_ref, o_ref, *, d):
    x = x_ref[...]
    h = _rms(x, gn_ref[...]).astype(BF16)
    m = None
    for i, (b_ref, w_ref) in enumerate(((og_ref, wg_ref), (or_ref, wr_ref), (os_ref, ws_ref))):
        gate = _sigmoid(_dot(h, wt_ref[:, i * d:(i + 1) * d]))
        term = gate * _dot(b_ref[...], w_ref[...])
        m = term if m is None else m + term
    o_ref[...] = x + _dot(m.astype(BF16), wo_ref[...])


def _merge(x, og, orw, os5, gn, wt, wg, wr, ws, wo):
    m, d = x.shape
    tm = _pick_block(m, 16, 512)
    return pl.pallas_call(
        functools.partial(_merge_body, d=d),
        grid=(m // tm,),
        in_specs=[_row_spec(tm, d), _row_spec(tm, GLA_V), _row_spec(tm, RWKV_W), _row_spec(tm, S5_WIDTH),
                  _const_spec((1, d))] + [_const_spec(w.shape) for w in (wt, wg, wr, ws, wo)],
        out_specs=_row_spec(tm, d),
        out_shape=jax.ShapeDtypeStruct((m, d), F32),
        compiler_params=_params(("parallel",)),
        name="merge",
    )(x, og, orw, os5, gn, wt, wg, wr, ws, wo)


def _ffn_body(x_ref, g_ref, w1_ref, w3_ref, w2_ref, o_ref, *, dff, cf):
    x = x_ref[...]
    h = _rms(x, g_ref[...]).astype(BF16)
    acc = x
    for c0 in range(0, dff, cf):
        a = _dot(h, w1_ref[:, c0:c0 + cf])
        b = _dot(h, w3_ref[:, c0:c0 + cf])
        acc = acc + _dot((a * _sigmoid(a) * b).astype(BF16), w2_ref[c0:c0 + cf, :])
    o_ref[...] = acc


def _ffn(x, g, w1, w3, w2):
    m, d = x.shape
    tm = _pick_block(m, SUBLANES, 1024)
    dff = w1.shape[1]
    cf = _pick_block(dff, LANES, 512)
    return pl.pallas_call(
        functools.partial(_ffn_body, dff=dff, cf=cf),
        grid=(m // tm,),
        in_specs=[_row_spec(tm, d), _const_spec((1, d)), _const_spec(w1.shape), _const_spec(w3.shape),
                  _const_spec(w2.shape)],
        out_specs=_row_spec(tm, d),
        out_shape=jax.ShapeDtypeStruct((m, d), F32),
        compiler_params=_params(("parallel",)),
        name="ffn",
    )(x, g, w1, w3, w2)


def _final_norm_body(x_ref, g_ref, o_ref):
    o_ref[...] = _rms(x_ref[...], g_ref[...])


def _final_norm(x, g):
    m, d = x.shape
    tm = _pick_block(m, SUBLANES, 1024)
    return pl.pallas_call(
        _final_norm_body,
        grid=(m // tm,),
        in_specs=[_row_spec(tm, d), _const_spec((1, d))],
        out_specs=_row_spec(tm, d),
        out_shape=jax.ShapeDtypeStruct((m, d), F32),
        compiler_params=_params(("parallel",)),
        name="final_norm",
    )(x, g)


def _block_ones(n, blk, dtype):
    i = jnp.arange(n) // blk
    return (i[:, None] == i[None, :]).astype(dtype)


def _gla_consts():
    c = GLA_SUB
    i = jnp.arange(GLA_TILE)
    same = i[:, None] // c == i[None, :] // c
    tri = ((i[:, None] >= i[None, :]) & same).astype(F32)
    tot = same.astype(F32)
    first = (jnp.arange(GLA_TILE // c)[:, None] * c == i[None, :]).astype(F32)
    segb = (jnp.arange(GLA_QK)[:, None] // GLA_DK == jnp.arange(GLA_V)[None, :] // GLA_DV).astype(BF16)
    sel = (jnp.arange(16)[:, None] // 3 == jnp.arange(3 * GLA_DV)[None, :] // GLA_DV).astype(BF16)
    return (tri, tot, first, segb), sel


def _block_diag_slabs(p, nslab):
    g, a, b = p.shape
    gs = g // nslab
    eye = jnp.eye(gs, dtype=p.dtype)
    p = p.reshape(nslab, gs, a, b)
    return (p[:, :, :, None, :] * eye[None, :, None, :, None]).reshape(nslab, gs * a, gs * b)


def kernel(x_prompt, x_sample, state_gla, state_rwkv, state_rwkv_shift, state_s5_re, state_s5_im, meta_tokens, norm_mix, norm_ffn, w_in, gla_gate_w2, gla_gate_b, gla_norm, rwkv_mu, rwkv_w0, rwkv_w2, rwkv_a0, rwkv_a2, rwkv_g2, rwkv_k_k, rwkv_k_a, rwkv_r_k, rwkv_ln_w, rwkv_ln_b, s5_a_re, s5_a_im, s5_log_dt, s5_b_re, s5_b_im, s5_c_re, s5_c_im, s5_d, s5_glu_w, s5_glu_b, w_br_gla, w_br_rwkv, w_br_s5, w_out, ffn_w1, ffn_w3, ffn_w2, final_norm):
    nb, seq, d = x_prompt.shape
    ns = x_sample.shape[0]
    assert x_sample.shape[1] == 1
    depth = w_in.shape[0]
    t = seq + N_META
    tp = -(-t // LANES) * LANES
    mp = nb * tp
    assert nb * RWKV_HEADS == CHAINS and (ns * RWKV_HEADS) % CHAINS == 0

    meta = jnp.broadcast_to(meta_tokens.astype(x_prompt.dtype)[None], (nb, N_META, d))
    xp = jnp.concatenate([meta, x_prompt, jnp.zeros((nb, tp - t, d), x_prompt.dtype)], axis=1).reshape(mp, d)
    xs = x_sample.reshape(ns, d)

    gla_consts, sel = _gla_consts()
    bd = _block_ones(RWKV_W, RWKV_N, BF16)
    o_q, o_z = 0, 2 * GLA_QK + 2 * GLA_V
    o_r = o_z + GLA_GATE_RANK
    o_s = o_r + RWKV_COLS
    o_t = o_s + S5_WIDTH

    new_p = [[] for _ in range(5)]
    new_s = [[] for _ in range(5)]
    for l in range(depth):
        wl = w_in[l]
        wg = jnp.concatenate([wl[:, o_q:o_r], jnp.zeros((d, GLA_ZPAD - GLA_GATE_RANK), F32)], axis=1).astype(BF16)
        w_groups = (wg, wl[:, o_r:o_s].astype(BF16), wl[:, o_s:o_t].astype(BF16))
        w_gates = wl[:, o_t:].astype(BF16)
        pg_p, pr_p, ps_p = _inproj(xp, norm_mix[l][None], w_groups)
        pg_s, pr_s, ps_s = _inproj(xs, norm_mix[l][None], w_groups)

        w2p = jnp.concatenate([gla_gate_w2[l], jnp.zeros((GLA_ZPAD - GLA_GATE_RANK, GLA_QK), F32)], axis=0).astype(BF16)
        gb = gla_gate_b[l][None]
        gn = gla_norm[l][None]
        og_p, gla_p = _gla_state(*_gla_intra(pg_p, w2p, gb, gn, gla_consts), nb, t, tp)
        gla_p = jnp.swapaxes(gla_p[:, :, :, :GLA_DK], 2, 3)
        og_s, gla_s = _gla_sample(pg_s, state_gla[l].reshape(ns, GLA_QK, GLA_DV), w2p, gb, gn, sel)
        new_p[0].append(gla_p.reshape(nb, GLA_HEADS, GLA_DK, GLA_DV))
        new_s[0].append(gla_s.reshape(ns, GLA_HEADS, GLA_DK, GLA_DV))

        wlow = jnp.zeros((RWKV_LOW, 3 * RWKV_W), F32)
        wlow = wlow.at[0:RWKV_W_RANK, 0:RWKV_W].set(rwkv_w2[l])
        wlow = wlow.at[RWKV_W_RANK:RWKV_W_RANK + RWKV_A_RANK, RWKV_W:2 * RWKV_W].set(rwkv_a2[l])
        wlow = wlow.at[RWKV_W_RANK + RWKV_A_RANK:, 2 * RWKV_W:].set(rwkv_g2[l])
        prep_consts = (rwkv_mu[l][None], rwkv_w0[l][None], rwkv_a0[l][None], rwkv_k_k[l][None], rwkv_k_a[l][None],
                       rwkv_r_k[l].reshape(1, RWKV_W), wlow.astype(BF16), bd)
        post_consts = (rwkv_ln_w[l][None], rwkv_ln_b[l][None], bd)
        (bonus, g_), tr = _rwkv_prep_seq(pr_p, prep_consts, nb, tp)
        chain = _to_chain(tr[:3], nb, tp) + _to_chain(tr[3:], nb, tp)
        yt, sf = _rwkv_scan(*chain, _state_to_chains(jnp.zeros((nb, RWKV_HEADS, RWKV_N, RWKV_N), F32), nb), t)
        orw_p = _rwkv_post(_from_chain(yt.reshape(tp * HALF_N, LANES), nb, tp), bonus, g_, *post_consts)
        new_p[1].append(_state_from_chains(sf, nb))
        new_p[2].append(pr_p.reshape(nb, tp, RWKV_COLS)[:, t - 1])
        *scan_ops, bonus, g_ = _rwkv_prep_step(pr_s, state_rwkv_shift[l], prep_consts)
        chain = [_sample_to_chain(a, ns) for a in scan_ops]
        yt, sf = _rwkv_scan(*chain, _state_to_chains(state_rwkv[l], ns), 1)
        orw_s = _rwkv_post(_sample_from_chain(yt, ns), bonus, g_, *post_consts)
        new_s[1].append(_state_from_chains(sf, ns))
        new_s[2].append(pr_s)

        lre = s5_a_re[l].reshape(1, S5_N)
        lim = s5_a_im[l].reshape(1, S5_N)
        ldt = jnp.broadcast_to(s5_log_dt[l][:, None], (S5_GROUPS, S5_STATE)).reshape(1, S5_N)
        bre = _block_diag_slabs(jnp.swapaxes(s5_b_re[l], 1, 2), S5_SLABS)
        bim = _block_diag_slabs(jnp.swapaxes(s5_b_im[l], 1, 2), S5_SLABS)
        cre = _block_diag_slabs(jnp.swapaxes(s5_c_re[l], 1, 2), S5_SLABS).astype(BF16)
        cim = _block_diag_slabs(jnp.swapaxes(s5_c_im[l], 1, 2), S5_SLABS).astype(BF16)
        s5_args = (lre, lim, ldt, bre, bim, cre, cim, s5_d[l][None], s5_glu_w[l].astype(BF16), s5_glu_b[l][None])
        os_p, xf_p = _s5(ps_p.reshape(nb, tp, S5_WIDTH), jnp.zeros((nb, 2 * S5_N), F32), *s5_args, nb, t)
        os_p = os_p.reshape(mp, S5_WIDTH)
        x0_s = jnp.concatenate([state_s5_re[l].reshape(ns, S5_N), state_s5_im[l].reshape(ns, S5_N)], axis=1)
        os_s, xf_s = _s5(ps_s.reshape(ns, 1, S5_WIDTH), x0_s, *s5_args, ns, 1)
        os_s = os_s.reshape(ns, S5_WIDTH)
        new_p[3].append(xf_p[:, :S5_N].reshape(nb, S5_GROUPS, S5_STATE))
        new_p[4].append(xf_p[:, S5_N:].reshape(nb, S5_GROUPS, S5_STATE))
        new_s[3].append(xf_s[:, :S5_N].reshape(ns, S5_GROUPS, S5_STATE))
        new_s[4].append(xf_s[:, S5_N:].reshape(ns, S5_GROUPS, S5_STATE))

        w_merge = (w_br_gla[l].astype(BF16), w_br_rwkv[l].astype(BF16), w_br_s5[l].astype(BF16), w_out[l].astype(BF16))
        w_ffn = (norm_ffn[l][None], ffn_w1[l].astype(BF16), ffn_w3[l].astype(BF16), ffn_w2[l].astype(BF16))
        xp = _ffn(_merge(xp, og_p, orw_p, os_p, norm_mix[l][None], w_gates, *w_merge), *w_ffn)
        xs = _ffn(_merge(xs, og_s, orw_s, os_s, norm_mix[l][None], w_gates, *w_merge), *w_ffn)

    y_prompt = _final_norm(xp, final_norm[None]).reshape(nb, tp, d)[:, N_META:t]
    y_sample = _final_norm(xs, final_norm[None]).reshape(ns, 1, d)
    outs_p = [jnp.stack(a, axis=0) for a in new_p]
    outs_s = [jnp.stack(a, axis=0) for a in new_s]
    return (y_prompt, y_sample, *outs_p, *outs_s)
```
